```python
import math
import jax, jax.numpy as jnp
from jax import lax
import numpy as np

D_MODEL = 2048
BATCH = 4
SEQ = 2048
DEPTH = 2
DEC_BATCH = 32
DEC_SEQ = 32
PAST_LEN = 4096

CHUNK = 64
N_MIXERS = 2
N_A_LAYERS = (DEPTH + 1) // 2
N_B_LAYERS = DEPTH // 2

GMLP_CHUNK = 128
GMLP_HALF = D_MODEL
GMLP_GROUPS = 8
GMLP_GROUP_DIM = GMLP_HALF // GMLP_GROUPS

RET_HEADS = 8
RET_DK = D_MODEL // RET_HEADS
RET_DV = 2 * RET_DK
RET_QK = RET_HEADS * RET_DK
RET_V = RET_HEADS * RET_DV
ROPE_BASE = 10000.0

FFN_HIDDEN = ((8 * D_MODEL + 3 * 256 - 1) // (3 * 256)) * 256

ALPHA = (2 * DEPTH) ** 0.25
BETA = (8 * DEPTH) ** -0.25
LN_EPS = 1e-5
GN_EPS = 1e-6

kernel_name = "hybrid_gmlp_retention_stream_step"


def layer_norm(x, g, b):
    xf = x.astype(jnp.float32)
    mu = jnp.mean(xf, axis=-1, keepdims=True)
    var = jnp.mean(jnp.square(xf - mu), axis=-1, keepdims=True)
    y = (xf - mu) * lax.rsqrt(var + LN_EPS) * g.astype(jnp.float32) + b.astype(jnp.float32)
    return y.astype(x.dtype)


def gmlp_mix(x, w_in, ln_g, ln_b, w_s, b_s, w_out):
    bsz, t, _ = x.shape
    seg = min(t, GMLP_CHUNK)
    n = t // seg
    z = jax.nn.gelu(x @ w_in, approximate=False)
    u, v = jnp.split(z, 2, axis=-1)
    v = layer_norm(v, ln_g, ln_b)
    mask = jnp.tril(jnp.ones((seg, seg), dtype=bool))
    w = jnp.where(mask[None], w_s[:, :seg, :seg], 0.0).astype(x.dtype)
    vg = v.reshape(bsz, n, seg, GMLP_GROUPS, GMLP_GROUP_DIM)
    bias = jnp.swapaxes(b_s[:, :seg], 0, 1)[None, None, :, :, None].astype(x.dtype)
    mixed = jnp.einsum('gts,bnsgc->bntgc', w, vg) + bias
    s = u * mixed.reshape(bsz, t, GMLP_HALF)
    return s @ w_out, v


def rope(x, pos):
    half = RET_DK // 2
    inv = jnp.power(ROPE_BASE, -jnp.arange(half, dtype=jnp.float32) / half)
    ang = pos.astype(jnp.float32)[:, None] * inv[None, :]
    cos, sin = jnp.cos(ang), jnp.sin(ang)
    x1, x2 = x[..., :half], x[..., half:]
    return jnp.concatenate([x1 * cos - x2 * sin, x1 * sin + x2 * cos], axis=-1)


def retention_block(q, k, v, s0, decay, inner, zeta, blk_decay):
    scores = jnp.einsum('bhqd,bhkd->bhqk', q, k) * decay[None]
    intra = jnp.einsum('bhqk,bhkv->bhqv', scores, v)
    cross = jnp.einsum('bhqd,bhdv->bhqv', q, s0) * inner[None, :, :, None]
    s_new = blk_decay[None, :, None, None] * s0 + jnp.einsum(
        'bhkd,bhkv->bhdv', k * zeta[None, :, :, None], v)
    return intra + cross, s_new


def retention_mix(x, pos, s0, w_in, w_out):
    bsz, t, _ = x.shape
    proj = x @ w_in
    q, k, v, g = jnp.split(proj, [RET_QK, 2 * RET_QK, 2 * RET_QK + RET_V], axis=-1)

    def heads(a, d):
        return a.reshape(bsz, t, RET_HEADS, d).transpose(0, 2, 1, 3).astype(jnp.float32)

    q = rope(heads(q, RET_DK), pos) * (RET_DK ** -0.5)
    k = rope(heads(k, RET_DK), pos)
    v = heads(v, RET_DV)

    blk = min(t, CHUNK)
    n = t // blk
    log_gamma = jnp.log1p(-jnp.power(2.0, -5.0 - jnp.arange(RET_HEADS, dtype=jnp.float32)))
    idx = jnp.arange(blk, dtype=jnp.float32)
    diff = idx[:, None] - idx[None, :]
    decay = jnp.where(diff[None] >= 0,
                      jnp.exp(log_gamma[:, None, None] * jnp.maximum(diff, 0.0)[None]), 0.0)
    inner = jnp.exp(log_gamma[:, None] * (idx + 1.0)[None])
    zeta = jnp.exp(log_gamma[:, None] * (blk - 1.0 - idx)[None])
    blk_decay = jnp.exp(log_gamma * blk)

    def to_blocks(a):
        return a.reshape(bsz, RET_HEADS, n, blk, a.shape[-1]).transpose(2, 0, 1, 3, 4)

    def step(s, qkv):
        o, s = retention_block(qkv[0], qkv[1], qkv[2], s, decay, inner, zeta, blk_decay)
        return s, o

    s_final, o = lax.scan(step, s0.astype(jnp.float32), (to_blocks(q), to_blocks(k), to_blocks(v)))
    o = o.transpose(1, 2, 0, 3, 4).reshape(bsz, RET_HEADS, t, RET_DV)
    mu = jnp.mean(o, axis=-1, keepdims=True)
    var = jnp.mean(jnp.square(o - mu), axis=-1, keepdims=True)
    o = (o - mu) * lax.rsqrt(var + GN_EPS)
    o = o.transpose(0, 2, 1, 3).reshape(bsz, t, RET_V)
    y = (jax.nn.silu(g.astype(jnp.float32)) * o).astype(x.dtype) @ w_out
    return y, s_final.astype(x.dtype)


def swiglu(x, w_in, w_out):
    gate, up = jnp.split(x @ w_in, 2, axis=-1)
    return (jax.nn.silu(gate) * up) @ w_out


def trunk(x, pos, states, w_a_in, a_ln_g, a_ln_b, a_ws, a_bs, w_a_out, w_b_in, w_b_out,
          w_ffn_in, w_ffn_out, ln_mix_g, ln_mix_b, ln_ffn_g, ln_ffn_b):
    new_states, v_rows = [], []
    for i in range(DEPTH):
        j = i // N_MIXERS
        if i % N_MIXERS == 0:
            h, v = gmlp_mix(x, w_a_in[j], a_ln_g[j], a_ln_b[j], a_ws[j], a_bs[j], w_a_out[j])
            v_rows.append(v)
        else:
            h, s = retention_mix(x, pos, states[j], w_b_in[j], w_b_out[j])
            new_states.append(s)
        x = layer_norm(ALPHA * x + h, ln_mix_g[i], ln_mix_b[i])
        x = layer_norm(ALPHA * x + swiglu(x, w_ffn_in[i], w_ffn_out[i]), ln_ffn_g[i], ln_ffn_b[i])
    return x, jnp.stack(new_states), jnp.stack(v_rows)


def setup_inputs(seed: int = 0) -> dict:
    key = jax.random.key(seed)
    ks = jax.random.split(key, 20)
    f32 = jnp.float32
    nrm = lambda k, shape, scale: jax.random.normal(k, shape, f32) * scale
    return {
        "x_prompt": nrm(ks[0], (BATCH, SEQ, D_MODEL), 1.0),
        "x_sample": nrm(ks[1], (DEC_BATCH, DEC_SEQ, D_MODEL), 1.0),
        "state_ret": nrm(ks[2], (N_B_LAYERS, DEC_BATCH, RET_HEADS, RET_DK, RET_DV), 1.0),
        "w_a_in": nrm(ks[3], (N_A_LAYERS, D_MODEL, 2 * GMLP_HALF), D_MODEL ** -0.5),
        "a_ln_g": 1.0 + nrm(ks[4], (N_A_LAYERS, GMLP_HALF), 0.02),
        "a_ln_b": nrm(ks[5], (N_A_LAYERS, GMLP_HALF), 0.02),
        "a_ws": nrm(ks[6], (N_A_LAYERS, GMLP_GROUPS, GMLP_CHUNK, GMLP_CHUNK), GMLP_CHUNK ** -0.5),
        "a_bs": 1.0 + nrm(ks[7], (N_A_LAYERS, GMLP_GROUPS, GMLP_CHUNK), 0.1),
        "w_a_out": nrm(ks[8], (N_A_LAYERS, GMLP_HALF, D_MODEL), BETA * GMLP_HALF ** -0.5),
        "w_b_in": nrm(ks[9], (N_B_LAYERS, D_MODEL, 2 * RET_QK + 2 * RET_V), D_MODEL ** -0.5),
        "w_b_out": nrm(ks[10], (N_B_LAYERS, RET_V, D_MODEL), BETA * RET_V ** -0.5),
        "w_ffn_in": nrm(ks[11], (DEPTH, D_MODEL, 2 * FFN_HIDDEN), D_MODEL ** -0.5),
        "w_ffn_out": nrm(ks[12], (DEPTH, FFN_HIDDEN, D_MODEL), BETA * FFN_HIDDEN ** -0.5),
        "ln_mix_g": 1.0 + nrm(ks[13], (DEPTH, D_MODEL), 0.02),
        "ln_mix_b": nrm(ks[14], (DEPTH, D_MODEL), 0.02),
        "ln_ffn_g": 1.0 + nrm(ks[15], (DEPTH, D_MODEL), 0.02),
        "ln_ffn_b": nrm(ks[16], (DEPTH, D_MODEL), 0.02),
    }


def reference(x_prompt, x_sample, state_ret, w_a_in, a_ln_g, a_ln_b, a_ws, a_bs, w_a_out,
              w_b_in, w_b_out, w_ffn_in, w_ffn_out, ln_mix_g, ln_mix_b, ln_ffn_g, ln_ffn_b):
    pos_prompt = jnp.arange(x_prompt.shape[1], dtype=jnp.int32)
    pos_sample = PAST_LEN + jnp.arange(x_sample.shape[1], dtype=jnp.int32)
    zero_state = jnp.zeros((N_B_LAYERS, x_prompt.shape[0], RET_HEADS, RET_DK, RET_DV), x_prompt.dtype)
    y_prompt, ret_state_prompt, _ = trunk(
        x_prompt, pos_prompt, zero_state, w_a_in, a_ln_g, a_ln_b, a_ws, a_bs, w_a_out,
        w_b_in, w_b_out, w_ffn_in, w_ffn_out, ln_mix_g, ln_mix_b, ln_ffn_g, ln_ffn_b)
    y_sample, ret_state_sample, gmlp_v_sample = trunk(
        x_sample, pos_sample, state_ret, w_a_in, a_ln_g, a_ln_b, a_ws, a_bs, w_a_out,
        w_b_in, w_b_out, w_ffn_in, w_ffn_out, ln_mix_g, ln_mix_b, ln_ffn_g, ln_ffn_b)
    return (y_prompt, y_sample, ret_state_prompt, ret_state_sample, gmlp_v_sample)
```

```python
import functools
import math

import jax
import jax.numpy as jnp
import numpy as np
from jax import lax
from jax.experimental import pallas as pl
from jax.experimental.pallas import tpu as pltpu

D_MODEL = 2048
DEPTH = 2
PAST_LEN = 4096
RET_BLOCK = 64
GMLP_CHUNK = 128
GMLP_GROUPS = 8
GMLP_GROUP_DIM = D_MODEL // GMLP_GROUPS
RET_HEADS = 8
RET_DK = D_MODEL // RET_HEADS
RET_DV = 2 * RET_DK
RET_QK = RET_HEADS * RET_DK
RET_V = RET_HEADS * RET_DV
ROPE_BASE = 10000.0
ROPE_HALF = RET_DK // 2
FFN_HIDDEN = 5632
ALPHA = (2 * DEPTH) ** 0.25
LN_EPS = 1e-5
GN_EPS = 1e-6

F32 = jnp.float32
BF16 = jnp.bfloat16
MIB = 1024 * 1024


def _params(semantics, vmem_mib):
    return pltpu.CompilerParams(dimension_semantics=semantics,
                                vmem_limit_bytes=vmem_mib * MIB)


def _layer_norm(x, g, b):
    mu = jnp.mean(x, axis=-1, keepdims=True)
    xc = x - mu
    var = jnp.mean(xc * xc, axis=-1, keepdims=True)
    return xc * lax.rsqrt(var + LN_EPS) * g + b


def _gelu(x):
    return 0.5 * x * (1.0 + lax.erf(x * np.float32(math.sqrt(0.5))))


def _rope_table_kernel(cos_ref, sin_ref, *, rows, n_prompt_rows, seq, dec_seq):
    base = pl.program_id(0) * rows
    r = lax.broadcasted_iota(jnp.int32, (rows, ROPE_HALF), 0) + base
    j = lax.broadcasted_iota(jnp.int32, (rows, ROPE_HALF), 1)
    pos = jnp.where(r < n_prompt_rows, r & (seq - 1),
                    PAST_LEN + ((r - n_prompt_rows) & (dec_seq - 1)))
    inv = jnp.power(np.float32(ROPE_BASE), -j.astype(F32) / ROPE_HALF)
    ang = pos.astype(F32) * inv
    cos_ref[...] = jnp.cos(ang)
    sin_ref[...] = jnp.sin(ang)


def _rope_tables(m, n_prompt_rows, seq, dec_seq):
    rows = 1024
    assert m % rows == 0 and seq & (seq - 1) == 0 and dec_seq & (dec_seq - 1) == 0
    spec = pl.BlockSpec((rows, ROPE_HALF), lambda i: (i, 0))
    return pl.pallas_call(
        functools.partial(_rope_table_kernel, rows=rows, n_prompt_rows=n_prompt_rows,
                          seq=seq, dec_seq=dec_seq),
        grid=(m // rows,),
        out_specs=[spec, spec],
        out_shape=[jax.ShapeDtypeStruct((m, ROPE_HALF), F32)] * 2,
        compiler_params=_params(("parallel",), 16),
        name="rope_tables",
    )()


def _gmlp_in_kernel(x_ref, w_ref, g_ref, b_ref, z_ref):
    j = pl.program_id(1)
    acc = jnp.dot(x_ref[...].astype(BF16), w_ref[...], preferred_element_type=F32)
    z = _gelu(acc)

    @pl.when(j == 0)
    def _():
        z_ref[...] = z

    @pl.when(j == 1)
    def _():
        z_ref[...] = _layer_norm(z, g_ref[...], b_ref[...])


def _gmlp_in(x, w, ln_g, ln_b, tm=512):
    m, d = x.shape
    half = w.shape[1] // 2
    return pl.pallas_call(
        _gmlp_in_kernel,
        grid=(m // tm, 2),
        in_specs=[pl.BlockSpec((tm, d), lambda i, j: (i, 0)),
                  pl.BlockSpec((d, half), lambda i, j: (0, j)),
                  pl.BlockSpec((1, half), lambda i, j: (0, 0)),
                  pl.BlockSpec((1, half), lambda i, j: (0, 0))],
        out_specs=pl.BlockSpec((tm, half), lambda i, j: (i, j)),
        out_shape=jax.ShapeDtypeStruct((m, 2 * half), F32),
        compiler_params=_params(("parallel", "arbitrary"), 48),
        name="gmlp_in",
    )(x, w, ln_g, ln_b)


def _gmlp_gate_kernel(u_ref, v_ref, x_ref, wmix_ref, bias_ref, wo_ref, g_ref, b_ref,
                      o_ref, s_ref, *, tm, n_prompt_tiles, dec_seq):
    i = pl.program_id(0)
    row = lax.broadcasted_iota(jnp.int32, (GMLP_CHUNK, GMLP_CHUNK), 0)
    col = lax.broadcasted_iota(jnp.int32, (GMLP_CHUNK, GMLP_CHUNK), 1)
    seg_bits = jnp.where(i < n_prompt_tiles, ~(GMLP_CHUNK - 1), ~(dec_seq - 1))
    mask = (col <= row) & (((row ^ col) & seg_bits) == 0)
    for g in range(GMLP_GROUPS):
        w_g = jnp.where(mask, wmix_ref[0, g], 0.0).astype(BF16)
        cs = slice(g * GMLP_GROUP_DIM, (g + 1) * GMLP_GROUP_DIM)
        for s in range(tm // GMLP_CHUNK):
            rs = slice(s * GMLP_CHUNK, (s + 1) * GMLP_CHUNK)
            mixed = jnp.dot(w_g, v_ref[rs, cs].astype(BF16), preferred_element_type=F32)
            mixed = mixed + bias_ref[0, :, cs]
            s_ref[rs, cs] = (u_ref[rs, cs] * mixed).astype(BF16)
    h = jnp.dot(s_ref[...], wo_ref[...], preferred_element_type=F32)
    o_ref[...] = _layer_norm(ALPHA * x_ref[...] + h, g_ref[...], b_ref[...])


def _gmlp_gate(z, x, wmix, bias, w_out, ln_g, ln_b, n_prompt_rows, dec_seq, tm=256):
    m, d = x.shape
    n_prompt_tiles = n_prompt_rows // tm
    sel = lambda i: i // n_prompt_tiles
    assert (m - n_prompt_rows) <= n_prompt_rows
    return pl.pallas_call(
        functools.partial(_gmlp_gate_kernel, tm=tm, n_prompt_tiles=n_prompt_tiles,
                          dec_seq=dec_seq),
        grid=(m // tm,),
        in_specs=[pl.BlockSpec((tm, d), lambda i: (i, 0)),
                  pl.BlockSpec((tm, d), lambda i: (i, 1)),
                  pl.BlockSpec((tm, d), lambda i: (i, 0)),
                  pl.BlockSpec((1, GMLP_GROUPS, GMLP_CHUNK, GMLP_CHUNK),
                               lambda i: (sel(i), 0, 0, 0)),
                  pl.BlockSpec((1, GMLP_CHUNK, d), lambda i: (sel(i), 0, 0)),
                  pl.BlockSpec((d, d), lambda i: (0, 0)),
                  pl.BlockSpec((1, d), lambda i: (0, 0)),
                  pl.BlockSpec((1, d), lambda i: (0, 0))],
        out_specs=pl.BlockSpec((tm, d), lambda i: (i, 0)),
        out_shape=jax.ShapeDtypeStruct((m, d), F32),
        scratch_shapes=[pltpu.VMEM((tm, d), BF16)],
        compiler_params=_params(("parallel",), 48),
        name="gmlp_gate",
    )(z, z, x, wmix, bias, w_out, ln_g, ln_b)


def _ffn_kernel(x_ref, wg_ref, wu_ref, wo_ref, g_ref, b_ref, o_ref, xb_ref, acc_ref):
    h = pl.program_id(1)

    @pl.when(h == 0)
    def _():
        xb_ref[...] = x_ref[...].astype(BF16)
        acc_ref[...] = jnp.zeros_like(acc_ref)

    xb = xb_ref[...]
    gate = jnp.dot(xb, wg_ref[...], preferred_element_type=F32)
    up = jnp.dot(xb, wu_ref[...], preferred_element_type=F32)
    act = (jax.nn.silu(gate) * up).astype(BF16)
    acc_ref[...] += jnp.dot(act, wo_ref[...], preferred_element_type=F32)

    @pl.when(h == pl.num_programs(1) - 1)
    def _():
        o_ref[...] = _layer_norm(ALPHA * x_ref[...] + acc_ref[...], g_ref[...], b_ref[...])


def _ffn(x, w_in, w_out, ln_g, ln_b, tm=512, th=512):
    m, d = x.shape
    hidden = w_out.shape[0]
    nh = hidden // th
    return pl.pallas_call(
        _ffn_kernel,
        grid=(m // tm, nh),
        in_specs=[pl.BlockSpec((tm, d), lambda i, h: (i, 0)),
                  pl.BlockSpec((d, th), lambda i, h: (0, h)),
                  pl.BlockSpec((d, th), lambda i, h: (0, h + nh)),
                  pl.BlockSpec((th, d), lambda i, h: (h, 0)),
                  pl.BlockSpec((1, d), lambda i, h: (0, 0)),
                  pl.BlockSpec((1, d), lambda i, h: (0, 0))],
        out_specs=pl.BlockSpec((tm, d), lambda i, h: (i, 0)),
        out_shape=jax.ShapeDtypeStruct((m, d), F32),
        scratch_shapes=[pltpu.VMEM((tm, d), BF16), pltpu.VMEM((tm, d), F32)],
        compiler_params=_params(("parallel", "arbitrary"), 48),
        name="ffn",
    )(x, w_in, w_in, w_out, ln_g, ln_b)


def _ret_in_kernel(x_ref, w_ref, cos_ref, sin_ref, o_ref, xb_ref, *, tn):
    j = pl.program_id(1)

    @pl.when(j == 0)
    def _():
        xb_ref[...] = x_ref[...].astype(BF16)

    acc = jnp.dot(xb_ref[...], w_ref[...], preferred_element_type=F32)

    @pl.when(j < 2 * RET_QK // tn)
    def _():
        scale = jnp.where(j < RET_QK // tn, np.float32(RET_DK ** -0.5), np.float32(1.0))
        cos = cos_ref[...]
        sin = sin_ref[...]
        for hh in range(tn // RET_DK):
            a = acc[:, hh * RET_DK:hh * RET_DK + ROPE_HALF]
            b = acc[:, hh * RET_DK + ROPE_HALF:(hh + 1) * RET_DK]
            o_ref[:, hh * RET_DK:hh * RET_DK + ROPE_HALF] = (
                (a * cos - b * sin) * scale).astype(BF16)
            o_ref[:, hh * RET_DK + ROPE_HALF:(hh + 1) * RET_DK] = (
                (a * sin + b * cos) * scale).astype(BF16)

    @pl.when(j >= 2 * RET_QK // tn)
    def _():
        o_ref[...] = acc.astype(BF16)


def _ret_in(x, w, cos, sin, tm=512, tn=1024):
    m, d = x.shape
    n = w.shape[1]
    return pl.pallas_call(
        functools.partial(_ret_in_kernel, tn=tn),
        grid=(m // tm, n // tn),
        in_specs=[pl.BlockSpec((tm, d), lambda i, j: (i, 0)),
                  pl.BlockSpec((d, tn), lambda i, j: (0, j)),
                  pl.BlockSpec((tm, ROPE_HALF), lambda i, j: (i, 0)),
                  pl.BlockSpec((tm, ROPE_HALF), lambda i, j: (i, 0))],
        out_specs=pl.BlockSpec((tm, tn), lambda i, j: (i, j)),
        out_shape=jax.ShapeDtypeStruct((m, n), BF16),
        scratch_shapes=[pltpu.VMEM((tm, d), BF16)],
        compiler_params=_params(("parallel", "arbitrary"), 48),
        name="ret_in",
    )(x, w, cos, sin)


def _decay_tables(head, blk):
    def log_gamma(shape):
        hv = jnp.full(shape, head, dtype=jnp.int32).astype(F32)
        return jnp.log1p(-jnp.exp2(-5.0 - hv))

    row = lax.broadcasted_iota(jnp.int32, (blk, blk), 0).astype(F32)
    col = lax.broadcasted_iota(jnp.int32, (blk, blk), 1).astype(F32)
    diff = row - col
    decay = jnp.where(diff >= 0, jnp.exp(log_gamma((blk, blk)) * jnp.maximum(diff, 0.0)), 0.0)
    idx_v = lax.broadcasted_iota(jnp.int32, (blk, RET_DV), 0).astype(F32)
    idx_k = lax.broadcasted_iota(jnp.int32, (blk, RET_DK), 0).astype(F32)
    inner = jnp.exp(log_gamma((blk, RET_DV)) * (idx_v + 1.0))
    zeta = jnp.exp(log_gamma((blk, RET_DK)) * (blk - 1.0 - idx_k))
    blk_decay = jnp.exp(log_gamma((1, RET_DV)) * blk)
    return decay, inner, zeta, blk_decay


def _ret_block(qb, kb, vb, gb, state, tables):
    decay, inner, zeta, blk_decay = tables
    scores = lax.dot_general(qb, kb, (((1,), (1,)), ((), ())),
                             preferred_element_type=F32) * decay
    intra = jnp.dot(scores.astype(BF16), vb, preferred_element_type=F32)
    cross = jnp.dot(qb, state.astype(BF16), preferred_element_type=F32) * inner
    kz = (kb.astype(F32) * zeta).T.astype(BF16)
    new_state = blk_decay * state + jnp.dot(kz, vb, preferred_element_type=F32)
    o = intra + cross
    mu = jnp.mean(o, axis=-1, keepdims=True)
    oc = o - mu
    var = jnp.mean(oc * oc, axis=-1, keepdims=True)
    on = oc * lax.rsqrt(var + GN_EPS)
    y = (jax.nn.silu(gb.astype(F32)) * on).astype(BF16)
    return y, new_state


def _ret_prompt_kernel(q_ref, k_ref, v_ref, g_ref, y_ref, s_ref, *, blk, n_blk):
    tables = _decay_tables(pl.program_id(1), blk)
    s_ref[...] = jnp.zeros_like(s_ref)

    def body(t, carry):
        rows = pl.ds(pl.multiple_of(t * blk, blk), blk)
        y, new_state = _ret_block(q_ref[rows, :], k_ref[rows, :], v_ref[rows, :],
                                  g_ref[rows, :], s_ref[0, 0], tables)
        y_ref[rows, :] = y
        s_ref[0, 0] = new_state
        return carry

    lax.fori_loop(0, n_blk, body, 0)


def _ret_prompt(proj, batch, seq):
    blk = min(seq, RET_BLOCK)
    k_off = RET_QK // RET_DK
    v_off = 2 * RET_QK // RET_DV
    g_off = v_off + RET_V // RET_DV
    return pl.pallas_call(
        functools.partial(_ret_prompt_kernel, blk=blk, n_blk=seq // blk),
        grid=(batch, RET_HEADS),
        in_specs=[pl.BlockSpec((seq, RET_DK), lambda b, h: (b, h)),
                  pl.BlockSpec((seq, RET_DK), lambda b, h: (b, k_off + h)),
                  pl.BlockSpec((seq, RET_DV), lambda b, h: (b, v_off + h)),
                  pl.BlockSpec((seq, RET_DV), lambda b, h: (b, g_off + h))],
        out_specs=[pl.BlockSpec((seq, RET_DV), lambda b, h: (b, h)),
                   pl.BlockSpec((1, 1, RET_DK, RET_DV), lambda b, h: (b, h, 0, 0))],
        out_shape=[jax.ShapeDtypeStruct((batch * seq, RET_V), BF16),
                   jax.ShapeDtypeStruct((batch, RET_HEADS, RET_DK, RET_DV), F32)],
        compiler_params=_params(("parallel", "parallel"), 32),
        name="ret_prompt",
    )(proj, proj, proj, proj)


def _ret_sample_kernel(q_ref, k_ref, v_ref, g_ref, s0_ref, y_ref, s_ref, *, blk):
    tables = _decay_tables(pl.program_id(1), blk)
    y, new_state = _ret_block(q_ref[...], k_ref[...], v_ref[...], g_ref[...],
                              s0_ref[0, 0], tables)
    y_ref[...] = y
    s_ref[0, 0] = new_state


def _ret_sample(proj, state0, n_prompt_rows, dec_batch, dec_seq):
    assert dec_seq <= RET_BLOCK and n_prompt_rows % dec_seq == 0
    r_off = n_prompt_rows // dec_seq
    k_off = RET_QK // RET_DK
    v_off = 2 * RET_QK // RET_DV
    g_off = v_off + RET_V // RET_DV
    state_spec = pl.BlockSpec((1, 1, RET_DK, RET_DV), lambda b, h: (b, h, 0, 0))
    return pl.pallas_call(
        functools.partial(_ret_sample_kernel, blk=dec_seq),
        grid=(dec_batch, RET_HEADS),
        in_specs=[pl.BlockSpec((dec_seq, RET_DK), lambda b, h: (r_off + b, h)),
                  pl.BlockSpec((dec_seq, RET_DK), lambda b, h: (r_off + b, k_off + h)),
                  pl.BlockSpec((dec_seq, RET_DV), lambda b, h: (r_off + b, v_off + h)),
                  pl.BlockSpec((dec_seq, RET_DV), lambda b, h: (r_off + b, g_off + h)),
                  state_spec],
        out_specs=[pl.BlockSpec((dec_seq, RET_DV), lambda b, h: (b, h)), state_spec],
        out_shape=[jax.ShapeDtypeStruct((dec_batch * dec_seq, RET_V), BF16),
                   jax.ShapeDtypeStruct(state0.shape, F32)],
        compiler_params=_params(("parallel", "parallel"), 32),
        name="ret_sample",
    )(proj, proj, proj, proj, state0)


def _proj_ln_kernel(a_ref, w_ref, x_ref, g_ref, b_ref, o_ref, acc_ref):
    k = pl.program_id(1)

    @pl.when(k == 0)
    def _():
        acc_ref[...] = jnp.zeros_like(acc_ref)

    acc_ref[...] += jnp.dot(a_ref[...], w_ref[...], preferred_element_type=F32)

    @pl.when(k == pl.num_programs(1) - 1)
    def _():
        o_ref[...] = _layer_norm(ALPHA * x_ref[...] + acc_ref[...], g_ref[...], b_ref[...])


def _proj_ln(a, w, x, ln_g, ln_b, tm=512, tk=1024):
    m, kdim = a.shape
    d = w.shape[1]
    return pl.pallas_call(
        _proj_ln_kernel,
        grid=(m // tm, kdim // tk),
        in_specs=[pl.BlockSpec((tm, tk), lambda i, k: (i, k)),
                  pl.BlockSpec((tk, d), lambda i, k: (k, 0)),
                  pl.BlockSpec((tm, d), lambda i, k: (i, 0)),
                  pl.BlockSpec((1, d), lambda i, k: (0, 0)),
                  pl.BlockSpec((1, d), lambda i, k: (0, 0))],
        out_specs=pl.BlockSpec((tm, d), lambda i, k: (i, 0)),
        out_shape=jax.ShapeDtypeStruct((m, d), F32),
        scratch_shapes=[pltpu.VMEM((tm, d), F32)],
        compiler_params=_params(("parallel", "arbitrary"), 48),
        name="proj_ln",
    )(a, w, x, ln_g, ln_b)


def kernel(x_prompt, x_sample, state_ret, w_a_in, a_ln_g, a_ln_b, a_ws, a_bs, w_a_out,
           w_b_in, w_b_out, w_ffn_in, w_ffn_out, ln_mix_g, ln_mix_b, ln_ffn_g, ln_ffn_b):
    batch, seq, d = x_prompt.shape
    dec_batch, dec_seq, _ = x_sample.shape
    n_p = batch * seq
    n_s = dec_batch * dec_seq
    assert d == D_MODEL and seq % GMLP_CHUNK == 0 and GMLP_CHUNK % dec_seq == 0

    x = jnp.concatenate([x_prompt.reshape(n_p, d), x_sample.reshape(n_s, d)], axis=0)
    row2 = lambda a: a.reshape(1, -1)

    reps = GMLP_CHUNK // dec_seq
    wmix = jnp.stack([a_ws[0], jnp.tile(a_ws[0][:, :dec_seq, :dec_seq], (1, reps, reps))])
    bias_rows = jnp.stack([a_bs[0], jnp.tile(a_bs[0][:, :dec_seq], (1, reps))])
    bias = jnp.repeat(jnp.swapaxes(bias_rows, 1, 2), GMLP_GROUP_DIM, axis=2)
    z = _gmlp_in(x, w_a_in[0].astype(BF16), row2(a_ln_g[0]), row2(a_ln_b[0]))
    x = _gmlp_gate(z, x, wmix, bias, w_a_out[0].astype(BF16),
                   row2(ln_mix_g[0]), row2(ln_mix_b[0]), n_p, dec_seq)
    gmlp_v_sample = z[n_p:, D_MODEL:].reshape(1, dec_batch, dec_seq, D_MODEL)
    x = _ffn(x, w_ffn_in[0].astype(BF16), w_ffn_out[0].astype(BF16),
             row2(ln_ffn_g[0]), row2(ln_ffn_b[0]))

    cos, sin = _rope_tables(n_p + n_s, n_p, seq, dec_seq)
    proj = _ret_in(x, w_b_in[0].astype(BF16), cos, sin)
    y_p, state_p = _ret_prompt(proj, batch, seq)
    y_s, state_s = _ret_sample(proj, state_ret[0], n_p, dec_batch, dec_seq)
    y = jnp.concatenate([y_p, y_s], axis=0)
    x = _proj_ln(y, w_b_out[0].astype(BF16), x, row2(ln_mix_g[1]), row2(ln_mix_b[1]))
    x = _ffn(x, w_ffn_in[1].astype(BF16), w_ffn_out[1].astype(BF16),
             row2(ln_ffn_g[1]), row2(ln_ffn_b[1]))

    y_prompt = x[:n_p].reshape(batch, seq, d)
    y_sample = x[n_p:].reshape(dec_batch, dec_seq, d)
    return (y_prompt, y_sample, state_p[None], state_s[None], gmlp_v_sample)
```

```python
import functools
import math

import jax
import jax.numpy as jnp
import numpy as np
from jax import lax
from jax.experimental import pallas as pl
from jax.experimental.pallas import tpu as pltpu

D_MODEL = 2048
DEPTH = 2
PAST_LEN = 4096
RET_BLOCK = 64
RET_CHUNK = 256
GMLP_CHUNK = 128
GMLP_GROUPS = 8
GMLP_GROUP_DIM = D_MODEL // GMLP_GROUPS
RET_HEADS = 8
RET_DK = D_MODEL // RET_HEADS
RET_DV = 2 * RET_DK
RET_QK = RET_HEADS * RET_DK
RET_V = RET_HEADS * RET_DV
ROPE_BASE = 10000.0
ROPE_HALF = RET_DK // 2
FFN_HIDDEN = 5632
ALPHA = (2 * DEPTH) ** 0.25
LN_EPS = 1e-5
GN_EPS = 1e-6

F32 = jnp.float32
BF16 = jnp.bfloat16
MIB = 1024 * 1024


def _params(semantics, vmem_mib):
    return pltpu.CompilerParams(dimension_semantics=semantics,
                                vmem_limit_bytes=vmem_mib * MIB)


def _layer_norm(x, g, b):
    mu = jnp.mean(x, axis=-1, keepdims=True)
    xc = x - mu
    var = jnp.mean(xc * xc, axis=-1, keepdims=True)
    return xc * lax.rsqrt(var + LN_EPS) * g + b


def _gelu(x):
    return 0.5 * x * (1.0 + lax.erf(x * np.float32(math.sqrt(0.5))))


def _rope_table_kernel(cos_ref, sin_ref, *, rows, n_prompt_rows, seq, dec_seq):
    base = pl.program_id(0) * rows
    r = lax.broadcasted_iota(jnp.int32, (rows, ROPE_HALF), 0) + base
    j = lax.broadcasted_iota(jnp.int32, (rows, ROPE_HALF), 1)
    pos = jnp.where(r < n_prompt_rows, r & (seq - 1),
                    PAST_LEN + ((r - n_prompt_rows) & (dec_seq - 1)))
    inv = jnp.power(np.float32(ROPE_BASE), -j.astype(F32) / ROPE_HALF)
    ang = pos.astype(F32) * inv
    cos_ref[...] = jnp.cos(ang)
    sin_ref[...] = jnp.sin(ang)


def _rope_tables(m, n_prompt_rows, seq, dec_seq):
    rows = 1024
    assert m % rows == 0 and seq & (seq - 1) == 0 and dec_seq & (dec_seq - 1) == 0
    spec = pl.BlockSpec((rows, ROPE_HALF), lambda i: (i, 0))
    return pl.pallas_call(
        functools.partial(_rope_table_kernel, rows=rows, n_prompt_rows=n_prompt_rows,
                          seq=seq, dec_seq=dec_seq),
        grid=(m // rows,),
        out_specs=[spec, spec],
        out_shape=[jax.ShapeDtypeStruct((m, ROPE_HALF), F32)] * 2,
        compiler_params=_params(("parallel",), 16),
        name="rope_tables",
    )()


def _gmlp_in_kernel(x_ref, w_ref, g_ref, b_ref, z_ref):
    j = pl.program_id(1)
    acc = jnp.dot(x_ref[...].astype(BF16), w_ref[...], preferred_element_type=F32)
    z = _gelu(acc)

    @pl.when(j == 0)
    def _():
        z_ref[...] = z

    @pl.when(j == 1)
    def _():
        z_ref[...] = _layer_norm(z, g_ref[...], b_ref[...])


def _gmlp_in(x, w, ln_g, ln_b, tm=512):
    m, d = x.shape
    half = w.shape[1] // 2
    return pl.pallas_call(
        _gmlp_in_kernel,
        grid=(m // tm, 2),
        in_specs=[pl.BlockSpec((tm, d), lambda i, j: (i, 0)),
                  pl.BlockSpec((d, half), lambda i, j: (0, j)),
                  pl.BlockSpec((1, half), lambda i, j: (0, 0)),
                  pl.BlockSpec((1, half), lambda i, j: (0, 0))],
        out_specs=pl.BlockSpec((tm, half), lambda i, j: (i, j)),
        out_shape=jax.ShapeDtypeStruct((m, 2 * half), F32),
        compiler_params=_params(("parallel", "arbitrary"), 48),
        name="gmlp_in",
    )(x, w, ln_g, ln_b)


def _gmlp_gate_kernel(u_ref, v_ref, x_ref, wmix_ref, bias_ref, wo_ref, g_ref, b_ref,
                      o_ref, s_ref, *, tm, n_prompt_tiles, dec_seq):
    i = pl.program_id(0)
    row = lax.broadcasted_iota(jnp.int32, (GMLP_CHUNK, GMLP_CHUNK), 0)
    col = lax.broadcasted_iota(jnp.int32, (GMLP_CHUNK, GMLP_CHUNK), 1)
    seg_bits = jnp.where(i < n_prompt_tiles, ~(GMLP_CHUNK - 1), ~(dec_seq - 1))
    mask = (col <= row) & (((row ^ col) & seg_bits) == 0)
    for g in range(GMLP_GROUPS):
        w_g = jnp.where(mask, wmix_ref[0, g], 0.0).astype(BF16)
        cs = slice(g * GMLP_GROUP_DIM, (g + 1) * GMLP_GROUP_DIM)
        for s in range(tm // GMLP_CHUNK):
            rs = slice(s * GMLP_CHUNK, (s + 1) * GMLP_CHUNK)
            mixed = jnp.dot(w_g, v_ref[rs, cs].astype(BF16), preferred_element_type=F32)
            mixed = mixed + bias_ref[0, :, cs]
            s_ref[rs, cs] = (u_ref[rs, cs] * mixed).astype(BF16)
    h = jnp.dot(s_ref[...], wo_ref[...], preferred_element_type=F32)
    o_ref[...] = _layer_norm(ALPHA * x_ref[...] + h, g_ref[...], b_ref[...])


def _gmlp_gate(z, x, wmix, bias, w_out, ln_g, ln_b, n_prompt_rows, dec_seq, tm=256):
    m, d = x.shape
    n_prompt_tiles = n_prompt_rows // tm
    sel = lambda i: i // n_prompt_tiles
    assert (m - n_prompt_rows) <= n_prompt_rows
    return pl.pallas_call(
        functools.partial(_gmlp_gate_kernel, tm=tm, n_prompt_tiles=n_prompt_tiles,
                          dec_seq=dec_seq),
        grid=(m // tm,),
        in_specs=[pl.BlockSpec((tm, d), lambda i: (i, 0)),
                  pl.BlockSpec((tm, d), lambda i: (i, 1)),
                  pl.BlockSpec((tm, d), lambda i: (i, 0)),
                  pl.BlockSpec((1, GMLP_GROUPS, GMLP_CHUNK, GMLP_CHUNK),
                               lambda i: (sel(i), 0, 0, 0)),
                  pl.BlockSpec((1, GMLP_CHUNK, d), lambda i: (sel(i), 0, 0)),
                  pl.BlockSpec((d, d), lambda i: (0, 0)),
                  pl.BlockSpec((1, d), lambda i: (0, 0)),
                  pl.BlockSpec((1, d), lambda i: (0, 0))],
        out_specs=pl.BlockSpec((tm, d), lambda i: (i, 0)),
        out_shape=jax.ShapeDtypeStruct((m, d), F32),
        scratch_shapes=[pltpu.VMEM((tm, d), BF16)],
        compiler_params=_params(("parallel",), 48),
        name="gmlp_gate",
    )(z, z, x, wmix, bias, w_out, ln_g, ln_b)


def _ffn_kernel(x_ref, wg_ref, wu_ref, wo_ref, g_ref, b_ref, o_ref, xb_ref, acc_ref):
    h = pl.program_id(1)

    @pl.when(h == 0)
    def _():
        xb_ref[...] = x_ref[...].astype(BF16)
        acc_ref[...] = jnp.zeros_like(acc_ref)

    xb = xb_ref[...]
    gate = jnp.dot(xb, wg_ref[...], preferred_element_type=F32)
    up = jnp.dot(xb, wu_ref[...], preferred_element_type=F32)
    act = (jax.nn.silu(gate) * up).astype(BF16)
    acc_ref[...] += jnp.dot(act, wo_ref[...], preferred_element_type=F32)

    @pl.when(h == pl.num_programs(1) - 1)
    def _():
        o_ref[...] = _layer_norm(ALPHA * x_ref[...] + acc_ref[...], g_ref[...], b_ref[...])


def _ffn(x, w_in, w_out, ln_g, ln_b, tm=512, th=512):
    m, d = x.shape
    hidden = w_out.shape[0]
    nh = hidden // th
    return pl.pallas_call(
        _ffn_kernel,
        grid=(m // tm, nh),
        in_specs=[pl.BlockSpec((tm, d), lambda i, h: (i, 0)),
                  pl.BlockSpec((d, th), lambda i, h: (0, h)),
                  pl.BlockSpec((d, th), lambda i, h: (0, h + nh)),
                  pl.BlockSpec((th, d), lambda i, h: (h, 0)),
                  pl.BlockSpec((1, d), lambda i, h: (0, 0)),
                  pl.BlockSpec((1, d), lambda i, h: (0, 0))],
        out_specs=pl.BlockSpec((tm, d), lambda i, h: (i, 0)),
        out_shape=jax.ShapeDtypeStruct((m, d), F32),
        scratch_shapes=[pltpu.VMEM((tm, d), BF16), pltpu.VMEM((tm, d), F32)],
        compiler_params=_params(("parallel", "arbitrary"), 48),
        name="ffn",
    )(x, w_in, w_in, w_out, ln_g, ln_b)


def _ret_in_kernel(x_ref, w_ref, cos_ref, sin_ref, o_ref, xb_ref, *, tn):
    j = pl.program_id(1)

    @pl.when(j == 0)
    def _():
        xb_ref[...] = x_ref[...].astype(BF16)

    acc = jnp.dot(xb_ref[...], w_ref[...], preferred_element_type=F32)

    @pl.when(j < 2 * RET_QK // tn)
    def _():
        scale = jnp.where(j < RET_QK // tn, np.float32(RET_DK ** -0.5), np.float32(1.0))
        cos = cos_ref[...]
        sin = sin_ref[...]
        for hh in range(tn // RET_DK):
            a = acc[:, hh * RET_DK:hh * RET_DK + ROPE_HALF]
            b = acc[:, hh * RET_DK + ROPE_HALF:(hh + 1) * RET_DK]
            o_ref[:, hh * RET_DK:hh * RET_DK + ROPE_HALF] = (
                (a * cos - b * sin) * scale).astype(BF16)
            o_ref[:, hh * RET_DK + ROPE_HALF:(hh + 1) * RET_DK] = (
                (a * sin + b * cos) * scale).astype(BF16)

    @pl.when(j >= 2 * RET_QK // tn)
    def _():
        o_ref[...] = acc.astype(BF16)


def _ret_in(x, w, cos, sin, tm=512, tn=1024):
    m, d = x.shape
    n = w.shape[1]
    return pl.pallas_call(
        functools.partial(_ret_in_kernel, tn=tn),
        grid=(m // tm, n // tn),
        in_specs=[pl.BlockSpec((tm, d), lambda i, j: (i, 0)),
                  pl.BlockSpec((d, tn), lambda i, j: (0, j)),
                  pl.BlockSpec((tm, ROPE_HALF), lambda i, j: (i, 0)),
                  pl.BlockSpec((tm, ROPE_HALF), lambda i, j: (i, 0))],
        out_specs=pl.BlockSpec((tm, tn), lambda i, j: (i, j)),
        out_shape=jax.ShapeDtypeStruct((m, n), BF16),
        scratch_shapes=[pltpu.VMEM((tm, d), BF16)],
        compiler_params=_params(("parallel", "arbitrary"), 48),
        name="ret_in",
    )(x, w, cos, sin)


def _decay_tables(head, blk):
    def log_gamma(shape):
        hv = jnp.full(shape, head, dtype=jnp.int32).astype(F32)
        return jnp.log1p(-jnp.exp2(-5.0 - hv))

    row = lax.broadcasted_iota(jnp.int32, (blk, blk), 0).astype(F32)
    col = lax.broadcasted_iota(jnp.int32, (blk, blk), 1).astype(F32)
    diff = row - col
    decay = jnp.where(diff >= 0, jnp.exp(log_gamma((blk, blk)) * jnp.maximum(diff, 0.0)), 0.0)
    idx_v = lax.broadcasted_iota(jnp.int32, (blk, RET_DV), 0).astype(F32)
    idx_k = lax.broadcasted_iota(jnp.int32, (blk, RET_DK), 0).astype(F32)
    inner = jnp.exp(log_gamma((blk, RET_DV)) * (idx_v + 1.0))
    zeta = jnp.exp(log_gamma((blk, RET_DK)) * (blk - 1.0 - idx_k))
    blk_decay = jnp.exp(log_gamma((1, RET_DV)) * blk)
    return decay, inner, zeta, blk_decay


def _ret_block(qb, kb, vb, gb, state, tables):
    decay, inner, zeta, blk_decay = tables
    scores = lax.dot_general(qb, kb, (((1,), (1,)), ((), ())),
                             preferred_element_type=F32) * decay
    intra = jnp.dot(scores.astype(BF16), vb, preferred_element_type=F32)
    cross = jnp.dot(qb, state.astype(BF16), preferred_element_type=F32) * inner
    kz = (kb.astype(F32) * zeta).T.astype(BF16)
    new_state = blk_decay * state + jnp.dot(kz, vb, preferred_element_type=F32)
    o = intra + cross
    mu = jnp.mean(o, axis=-1, keepdims=True)
    oc = o - mu
    var = jnp.mean(oc * oc, axis=-1, keepdims=True)
    on = oc * lax.rsqrt(var + GN_EPS)
    y = (jax.nn.silu(gb.astype(F32)) * on).astype(BF16)
    return y, new_state


def _ret_prompt_kernel(q_ref, k_ref, v_ref, g_ref, y_ref, s_ref, *, blk, n_blk):
    tables = _decay_tables(pl.program_id(1), blk)
    state = jnp.zeros((RET_DK, RET_DV), F32)
    for t in range(n_blk):
        rows = slice(t * blk, (t + 1) * blk)
        y, state = _ret_block(q_ref[rows, :], k_ref[rows, :], v_ref[rows, :],
                              g_ref[rows, :], state, tables)
        y_ref[rows, :] = y
    s_ref[0, 0] = state


def _ret_prompt(proj, batch, seq):
    blk = min(seq, RET_CHUNK)
    k_off = RET_QK // RET_DK
    v_off = 2 * RET_QK // RET_DV
    g_off = v_off + RET_V // RET_DV
    return pl.pallas_call(
        functools.partial(_ret_prompt_kernel, blk=blk, n_blk=seq // blk),
        grid=(batch, RET_HEADS),
        in_specs=[pl.BlockSpec((seq, RET_DK), lambda b, h: (b, h)),
                  pl.BlockSpec((seq, RET_DK), lambda b, h: (b, k_off + h)),
                  pl.BlockSpec((seq, RET_DV), lambda b, h: (b, v_off + h)),
                  pl.BlockSpec((seq, RET_DV), lambda b, h: (b, g_off + h))],
        out_specs=[pl.BlockSpec((seq, RET_DV), lambda b, h: (b, h)),
                   pl.BlockSpec((1, 1, RET_DK, RET_DV), lambda b, h: (b, h, 0, 0))],
        out_shape=[jax.ShapeDtypeStruct((proj.shape[0], RET_V), BF16),
                   jax.ShapeDtypeStruct((batch, RET_HEADS, RET_DK, RET_DV), F32)],
        compiler_params=_params(("parallel", "parallel"), 40),
        name="ret_prompt",
    )(proj, proj, proj, proj)


def _ret_sample_kernel(q_ref, k_ref, v_ref, g_ref, s0_ref, y_in_ref, y_ref, s_ref, *, blk):
    del y_in_ref
    for h in range(RET_HEADS):
        ck = slice(h * RET_DK, (h + 1) * RET_DK)
        cv = slice(h * RET_DV, (h + 1) * RET_DV)
        y, new_state = _ret_block(q_ref[:, ck], k_ref[:, ck], v_ref[:, cv], g_ref[:, cv],
                                  s0_ref[0, h], _decay_tables(h, blk))
        y_ref[:, cv] = y
        s_ref[0, h] = new_state


def _ret_sample(proj, state0, y, n_prompt_rows, dec_batch, dec_seq):
    assert dec_seq <= RET_BLOCK and n_prompt_rows % dec_seq == 0
    r_off = n_prompt_rows // dec_seq
    state_spec = pl.BlockSpec((1, RET_HEADS, RET_DK, RET_DV), lambda s: (s, 0, 0, 0))
    return pl.pallas_call(
        functools.partial(_ret_sample_kernel, blk=dec_seq),
        grid=(dec_batch,),
        in_specs=[pl.BlockSpec((dec_seq, RET_QK), lambda s: (r_off + s, 0)),
                  pl.BlockSpec((dec_seq, RET_QK), lambda s: (r_off + s, 1)),
                  pl.BlockSpec((dec_seq, RET_V), lambda s: (r_off + s, 2 * RET_QK // RET_V)),
                  pl.BlockSpec((dec_seq, RET_V), lambda s: (r_off + s, 2 * RET_QK // RET_V + 1)),
                  state_spec,
                  pl.BlockSpec(memory_space=pl.ANY)],
        out_specs=[pl.BlockSpec((dec_seq, RET_V), lambda s: (r_off + s, 0)), state_spec],
        out_shape=[jax.ShapeDtypeStruct(y.shape, BF16),
                   jax.ShapeDtypeStruct(state0.shape, F32)],
        input_output_aliases={5: 0},
        compiler_params=_params(("arbitrary",), 40),
        name="ret_sample",
    )(proj, proj, proj, proj, state0, y)


def _proj_ln_kernel(a_ref, w_ref, x_ref, g_ref, b_ref, o_ref, acc_ref):
    k = pl.program_id(1)

    @pl.when(k == 0)
    def _():
        acc_ref[...] = jnp.zeros_like(acc_ref)

    acc_ref[...] += jnp.dot(a_ref[...], w_ref[...], preferred_element_type=F32)

    @pl.when(k == pl.num_programs(1) - 1)
    def _():
        o_ref[...] = _layer_norm(ALPHA * x_ref[...] + acc_ref[...], g_ref[...], b_ref[...])


def _proj_ln(a, w, x, ln_g, ln_b, tm=512, tk=1024):
    m, kdim = a.shape
    d = w.shape[1]
    return pl.pallas_call(
        _proj_ln_kernel,
        grid=(m // tm, kdim // tk),
        in_specs=[pl.BlockSpec((tm, tk), lambda i, k: (i, k)),
                  pl.BlockSpec((tk, d), lambda i, k: (k, 0)),
                  pl.BlockSpec((tm, d), lambda i, k: (i, 0)),
                  pl.BlockSpec((1, d), lambda i, k: (0, 0)),
                  pl.BlockSpec((1, d), lambda i, k: (0, 0))],
        out_specs=pl.BlockSpec((tm, d), lambda i, k: (i, 0)),
        out_shape=jax.ShapeDtypeStruct((m, d), F32),
        scratch_shapes=[pltpu.VMEM((tm, d), F32)],
        compiler_params=_params(("parallel", "arbitrary"), 48),
        name="proj_ln",
    )(a, w, x, ln_g, ln_b)


def kernel(x_prompt, x_sample, state_ret, w_a_in, a_ln_g, a_ln_b, a_ws, a_bs, w_a_out,
           w_b_in, w_b_out, w_ffn_in, w_ffn_out, ln_mix_g, ln_mix_b, ln_ffn_g, ln_ffn_b):
    batch, seq, d = x_prompt.shape
    dec_batch, dec_seq, _ = x_sample.shape
    n_p = batch * seq
    n_s = dec_batch * dec_seq
    assert d == D_MODEL and seq % GMLP_CHUNK == 0 and GMLP_CHUNK % dec_seq == 0

    x = jnp.concatenate([x_prompt.reshape(n_p, d), x_sample.reshape(n_s, d)], axis=0)
    row2 = lambda a: a.reshape(1, -1)

    reps = GMLP_CHUNK // dec_seq
    wmix = jnp.stack([a_ws[0], jnp.tile(a_ws[0][:, :dec_seq, :dec_seq], (1, reps, reps))])
    bias_rows = jnp.stack([a_bs[0], jnp.tile(a_bs[0][:, :dec_seq], (1, reps))])
    bias = jnp.repeat(jnp.swapaxes(bias_rows, 1, 2), GMLP_GROUP_DIM, axis=2)
    z = _gmlp_in(x, w_a_in[0].astype(BF16), row2(a_ln_g[0]), row2(a_ln_b[0]))
    x = _gmlp_gate(z, x, wmix, bias, w_a_out[0].astype(BF16),
                   row2(ln_mix_g[0]), row2(ln_mix_b[0]), n_p, dec_seq)
    gmlp_v_sample = z[n_p:, D_MODEL:].reshape(1, dec_batch, dec_seq, D_MODEL)
    x = _ffn(x, w_ffn_in[0].astype(BF16), w_ffn_out[0].astype(BF16),
             row2(ln_ffn_g[0]), row2(ln_ffn_b[0]))

    cos, sin = _rope_tables(n_p + n_s, n_p, seq, dec_seq)
    proj = _ret_in(x, w_b_in[0].astype(BF16), cos, sin)
    y, state_p = _ret_prompt(proj, batch, seq)
    y, state_s = _ret_sample(proj, state_ret[0], y, n_p, dec_batch, dec_seq)
    x = _proj_ln(y, w_b_out[0].astype(BF16), x, row2(ln_mix_g[1]), row2(ln_mix_b[1]))
    x = _ffn(x, w_ffn_in[1].astype(BF16), w_ffn_out[1].astype(BF16),
             row2(ln_ffn_g[1]), row2(ln_ffn_b[1]))

    y_prompt = x[:n_p].reshape(batch, seq, d)
    y_sample = x[n_p:].reshape(dec_batch, dec_seq, d)
    return (y_prompt, y_sample, state_p[None], state_s[None], gmlp_v_sample)
```

```python
import functools
import math

import jax
import jax.numpy as jnp
import numpy as np
from jax import lax
from jax.experimental import pallas as pl
from jax.experimental.pallas import tpu as pltpu

D_MODEL = 2048
DEPTH = 2
PAST_LEN = 4096
RET_BLOCK = 64
RET_CHUNK = 256
GMLP_CHUNK = 128
GMLP_GROUPS = 8
GMLP_GROUP_DIM = D_MODEL // GMLP_GROUPS
RET_HEADS = 8
RET_DK = D_MODEL // RET_HEADS
RET_DV = 2 * RET_DK
RET_QK = RET_HEADS * RET_DK
RET_V = RET_HEADS * RET_DV
ROPE_BASE = 10000.0
ROPE_HALF = RET_DK // 2
ALPHA = (2 * DEPTH) ** 0.25
LN_EPS = 1e-5
GN_EPS = 1e-6

F32 = jnp.float32
BF16 = jnp.bfloat16
MIB = 1024 * 1024
ANY_SPEC = pl.BlockSpec(memory_space=pl.ANY)


def _params(semantics, vmem_mib):
    return pltpu.CompilerParams(dimension_semantics=semantics,
                                vmem_limit_bytes=vmem_mib * MIB)


def _layer_norm(x, g, b):
    mu = jnp.mean(x, axis=-1, keepdims=True)
    xc = x - mu
    var = jnp.mean(xc * xc, axis=-1, keepdims=True)
    return xc * lax.rsqrt(var + LN_EPS) * g + b


def _gelu(x):
    return 0.5 * x * (1.0 + lax.erf(x * np.float32(math.sqrt(0.5))))


def _stage_weight(step, w_ref, wb_ref, n_chunks):
    rows = w_ref.shape[0]

    @pl.when(step < n_chunks)
    def _():
        r0 = pl.multiple_of(step * rows, rows)
        wb_ref[pl.ds(r0, rows), :] = w_ref[...].astype(BF16)


def _row_spec(d, layer):
    return pl.BlockSpec((None, 1, d), lambda s: (layer, 0, 0))


def _chunk_spec(layer, rows, cols, n_chunks):
    return pl.BlockSpec((None, rows, cols),
                        lambda s: (layer, jnp.minimum(s, n_chunks - 1), 0))


def _gmlp_kernel(*refs, tm, n_stage, is_sample, dec_seq):
    if is_sample:
        (x_ref, win_ref, wout_ref, lng_ref, lnb_ref, wmix_ref, bias_ref, g_ref, b_ref, _, _,
         x1_ref, x1b_ref, vn_ref, win_b, wout_b, u_ref, v_ref, s_ref) = refs
    else:
        (x_ref, win_ref, wout_ref, lng_ref, lnb_ref, wmix_ref, bias_ref, g_ref, b_ref,
         x1_ref, x1b_ref, win_b, wout_b, u_ref, v_ref, s_ref) = refs
    step = pl.program_id(0)
    _stage_weight(step, win_ref, win_b, n_stage)
    _stage_weight(step, wout_ref, wout_b, n_stage)

    @pl.when(step >= n_stage)
    def _():
        d = D_MODEL
        xf = x_ref[...]
        xb = xf.astype(BF16)
        u_ref[...] = _gelu(jnp.dot(xb, win_b[:, :d], preferred_element_type=F32))
        v = _layer_norm(_gelu(jnp.dot(xb, win_b[:, d:], preferred_element_type=F32)),
                        lng_ref[...], lnb_ref[...])
        if is_sample:
            vn_ref[...] = v
        v_ref[...] = v.astype(BF16)
        row = lax.broadcasted_iota(jnp.int32, (GMLP_CHUNK, GMLP_CHUNK), 0)
        col = lax.broadcasted_iota(jnp.int32, (GMLP_CHUNK, GMLP_CHUNK), 1)
        mask = col <= row
        if is_sample:
            mask = mask & (((row ^ col) & ~(dec_seq - 1)) == 0)
        for g in range(GMLP_GROUPS):
            w_g = jnp.where(mask, wmix_ref[g], 0.0).astype(BF16)
            cs = slice(g * GMLP_GROUP_DIM, (g + 1) * GMLP_GROUP_DIM)
            for s in range(tm // GMLP_CHUNK):
                rs = slice(s * GMLP_CHUNK, (s + 1) * GMLP_CHUNK)
                mixed = jnp.dot(w_g, v_ref[rs, cs], preferred_element_type=F32)
                mixed = mixed + bias_ref[:, cs]
                s_ref[rs, cs] = (u_ref[rs, cs] * mixed).astype(BF16)
        h = jnp.dot(s_ref[...], wout_b[...], preferred_element_type=F32)
        x1 = _layer_norm(ALPHA * xf + h, g_ref[...], b_ref[...])
        x1_ref[...] = x1
        x1b_ref[...] = x1.astype(BF16)


def _gmlp_layer(x, w_in, w_out, ln_g, ln_b, wmix, bias, mix_g, mix_b, m_total, row_off,
                dec_seq, carry=None, tm=256, n_stage=16):
    rows, d = x.shape
    is_sample = carry is not None
    n_tiles = rows // tm
    off = row_off // tm
    tile = lambda s: (jnp.maximum(s - n_stage, 0), 0)
    out_tile = lambda s: (off + jnp.maximum(s - n_stage, 0), 0)
    row0 = _row_spec(d, 0)
    in_specs = [pl.BlockSpec((tm, d), tile),
                _chunk_spec(0, d // n_stage, 2 * d, n_stage),
                _chunk_spec(0, d // n_stage, d, n_stage),
                row0, row0,
                pl.BlockSpec((GMLP_GROUPS, GMLP_CHUNK, GMLP_CHUNK), lambda s: (0, 0, 0)),
                pl.BlockSpec((GMLP_CHUNK, d), lambda s: (0, 0)),
                row0, row0]
    out_specs = [pl.BlockSpec((tm, d), out_tile), pl.BlockSpec((tm, d), out_tile)]
    out_shape = [jax.ShapeDtypeStruct((m_total, d), F32),
                 jax.ShapeDtypeStruct((m_total, d), BF16)]
    args = [x, w_in, w_out, ln_g, ln_b, wmix, bias, mix_g, mix_b]
    aliases = {}
    if is_sample:
        in_specs += [ANY_SPEC, ANY_SPEC]
        args += list(carry)
        aliases = {len(args) - 2: 0, len(args) - 1: 1}
        out_specs.append(pl.BlockSpec((tm, d), tile))
        out_shape.append(jax.ShapeDtypeStruct((rows, d), F32))
    return pl.pallas_call(
        functools.partial(_gmlp_kernel, tm=tm, n_stage=n_stage, is_sample=is_sample,
                          dec_seq=dec_seq),
        grid=(n_stage + n_tiles,),
        in_specs=in_specs,
        out_specs=out_specs,
        out_shape=out_shape,
        scratch_shapes=[pltpu.VMEM((d, 2 * d), BF16), pltpu.VMEM((d, d), BF16),
                        pltpu.VMEM((tm, d), F32), pltpu.VMEM((tm, d), BF16),
                        pltpu.VMEM((tm, d), BF16)],
        input_output_aliases=aliases,
        compiler_params=_params(("arbitrary",), 60),
        name="gmlp_sample" if is_sample else "gmlp_prompt",
    )(*args)


def _ffn_in_kernel(x_ref, wg_ref, wu_ref, o_ref, wg_b, wu_b):
    @pl.when(pl.program_id(1) == 0)
    def _():
        wg_b[...] = wg_ref[...].astype(BF16)
        wu_b[...] = wu_ref[...].astype(BF16)

    xb = x_ref[...]
    gate = jnp.dot(xb, wg_b[...], preferred_element_type=F32)
    up = jnp.dot(xb, wu_b[...], preferred_element_type=F32)
    o_ref[...] = (jax.nn.silu(gate) * up).astype(BF16)


def _ffn_in(xb, w_in, layer, tm=1152, th=512):
    m, d = xb.shape
    hidden = w_in.shape[2] // 2
    nh = hidden // th
    return pl.pallas_call(
        _ffn_in_kernel,
        grid=(nh, m // tm),
        in_specs=[pl.BlockSpec((tm, d), lambda j, i: (i, 0)),
                  pl.BlockSpec((None, d, th), lambda j, i: (layer, 0, j)),
                  pl.BlockSpec((None, d, th), lambda j, i: (layer, 0, j + nh))],
        out_specs=pl.BlockSpec((tm, th), lambda j, i: (i, j)),
        out_shape=jax.ShapeDtypeStruct((m, hidden), BF16),
        scratch_shapes=[pltpu.VMEM((d, th), BF16), pltpu.VMEM((d, th), BF16)],
        compiler_params=_params(("arbitrary", "arbitrary"), 48),
        name="ffn_in",
    )(xb, w_in, w_in)


def _proj_ln_kernel(a_ref, w_ref, x_ref, g_ref, b_ref, *rest, n_stage, n_prompt_tiles):
    step = pl.program_id(0)
    if n_prompt_tiles is None:
        o_ref, ob_ref, w_b = rest
    else:
        op_ref, os_ref, w_b = rest
    _stage_weight(step, w_ref, w_b, n_stage)

    def result():
        h = jnp.dot(a_ref[...], w_b[...], preferred_element_type=F32)
        return _layer_norm(ALPHA * x_ref[...] + h, g_ref[...], b_ref[...])

    if n_prompt_tiles is None:
        @pl.when(step >= n_stage)
        def _():
            x = result()
            o_ref[...] = x
            ob_ref[...] = x.astype(BF16)
    else:
        t = step - n_stage

        @pl.when((t >= 0) & (t < n_prompt_tiles))
        def _():
            op_ref[...] = result()

        @pl.when(t >= n_prompt_tiles)
        def _():
            os_ref[...] = result()


def _proj_ln(a, w, layer, x, ln_g, ln_b, ln_row, chunk_rows, split_rows=None, tm=256):
    m, kdim = a.shape
    d = w.shape[2]
    n_stage = kdim // chunk_rows
    tile = lambda s: (jnp.maximum(s - n_stage, 0), 0)
    in_specs = [pl.BlockSpec((tm, kdim), tile),
                _chunk_spec(layer, chunk_rows, d, n_stage),
                pl.BlockSpec((tm, d), tile),
                _row_spec(d, ln_row), _row_spec(d, ln_row)]
    if split_rows is None:
        n_prompt_tiles = None
        out_specs = [pl.BlockSpec((tm, d), tile), pl.BlockSpec((tm, d), tile)]
        out_shape = [jax.ShapeDtypeStruct((m, d), F32), jax.ShapeDtypeStruct((m, d), BF16)]
    else:
        n_prompt_tiles = split_rows // tm
        out_specs = [
            pl.BlockSpec((tm, d), lambda s: (jnp.clip(s - n_stage, 0, n_prompt_tiles - 1), 0)),
            pl.BlockSpec((tm, d), lambda s: (jnp.maximum(s - n_stage - n_prompt_tiles, 0), 0))]
        out_shape = [jax.ShapeDtypeStruct((split_rows, d), F32),
                     jax.ShapeDtypeStruct((m - split_rows, d), F32)]
    return pl.pallas_call(
        functools.partial(_proj_ln_kernel, n_stage=n_stage, n_prompt_tiles=n_prompt_tiles),
        grid=(n_stage + m // tm,),
        in_specs=in_specs,
        out_specs=out_specs,
        out_shape=out_shape,
        scratch_shapes=[pltpu.VMEM((kdim, d), BF16)],
        compiler_params=_params(("arbitrary",), 56),
        name="proj_ln_split" if split_rows else "proj_ln",
    )(a, w, x, ln_g, ln_b)


def _rope_table_kernel(cos_ref, sin_ref, *, rows, n_prompt_rows, seq, dec_seq):
    base = pl.program_id(0) * rows
    r = lax.broadcasted_iota(jnp.int32, (rows, ROPE_HALF), 0) + base
    j = lax.broadcasted_iota(jnp.int32, (rows, ROPE_HALF), 1)
    pos = jnp.where(r < n_prompt_rows, r & (seq - 1),
                    PAST_LEN + ((r - n_prompt_rows) & (dec_seq - 1)))
    inv = jnp.power(np.float32(ROPE_BASE), -j.astype(F32) / ROPE_HALF)
    ang = pos.astype(F32) * inv
    cos_ref[...] = jnp.cos(ang)
    sin_ref[...] = jnp.sin(ang)


def _rope_tables(m, n_prompt_rows, seq, dec_seq):
    rows = 1024
    assert m % rows == 0 and seq & (seq - 1) == 0 and dec_seq & (dec_seq - 1) == 0
    spec = pl.BlockSpec((rows, ROPE_HALF), lambda i: (i, 0))
    return pl.pallas_call(
        functools.partial(_rope_table_kernel, rows=rows, n_prompt_rows=n_prompt_rows,
                          seq=seq, dec_seq=dec_seq),
        grid=(m // rows,),
        out_specs=[spec, spec],
        out_shape=[jax.ShapeDtypeStruct((m, ROPE_HALF), F32)] * 2,
        compiler_params=_params(("parallel",), 16),
        name="rope_tables",
    )()


def _ret_in_kernel(x_ref, w_ref, cos_ref, sin_ref, o_ref, w_b, *, tn):
    j = pl.program_id(0)

    @pl.when(pl.program_id(1) == 0)
    def _():
        w_b[...] = w_ref[...].astype(BF16)

    acc = jnp.dot(x_ref[...], w_b[...], preferred_element_type=F32)

    @pl.when(j < 2 * RET_QK // tn)
    def _():
        scale = jnp.where(j < RET_QK // tn, np.float32(RET_DK ** -0.5), np.float32(1.0))
        cos = cos_ref[...]
        sin = sin_ref[...]
        for hh in range(tn // RET_DK):
            a = acc[:, hh * RET_DK:hh * RET_DK + ROPE_HALF]
            b = acc[:, hh * RET_DK + ROPE_HALF:(hh + 1) * RET_DK]
            o_ref[:, hh * RET_DK:hh * RET_DK + ROPE_HALF] = (
                (a * cos - b * sin) * scale).astype(BF16)
            o_ref[:, hh * RET_DK + ROPE_HALF:(hh + 1) * RET_DK] = (
                (a * sin + b * cos) * scale).astype(BF16)

    @pl.when(j >= 2 * RET_QK // tn)
    def _():
        o_ref[...] = acc.astype(BF16)


def _ret_in(xb, w, cos, sin, tm=1152, tn=1024):
    m, d = xb.shape
    n = w.shape[2]
    return pl.pallas_call(
        functools.partial(_ret_in_kernel, tn=tn),
        grid=(n // tn, m // tm),
        in_specs=[pl.BlockSpec((tm, d), lambda j, i: (i, 0)),
                  pl.BlockSpec((None, d, tn), lambda j, i: (0, 0, j)),
                  pl.BlockSpec((tm, ROPE_HALF), lambda j, i: (i, 0)),
                  pl.BlockSpec((tm, ROPE_HALF), lambda j, i: (i, 0))],
        out_specs=pl.BlockSpec((tm, tn), lambda j, i: (i, j)),
        out_shape=jax.ShapeDtypeStruct((m, n), BF16),
        scratch_shapes=[pltpu.VMEM((d, tn), BF16)],
        compiler_params=_params(("arbitrary", "arbitrary"), 48),
        name="ret_in",
    )(xb, w, cos, sin)


def _decay_tables(head, blk):
    def log_gamma(shape):
        hv = jnp.full(shape, head, dtype=jnp.int32).astype(F32)
        return jnp.log1p(-jnp.exp2(-5.0 - hv))

    row = lax.broadcasted_iota(jnp.int32, (blk, blk), 0).astype(F32)
    col = lax.broadcasted_iota(jnp.int32, (blk, blk), 1).astype(F32)
    diff = row - col
    decay = jnp.where(diff >= 0, jnp.exp(log_gamma((blk, blk)) * jnp.maximum(diff, 0.0)), 0.0)
    idx_v = lax.broadcasted_iota(jnp.int32, (blk, RET_DV), 0).astype(F32)
    idx_k = lax.broadcasted_iota(jnp.int32, (blk, RET_DK), 0).astype(F32)
    inner = jnp.exp(log_gamma((blk, RET_DV)) * (idx_v + 1.0))
    zeta = jnp.exp(log_gamma((blk, RET_DK)) * (blk - 1.0 - idx_k))
    blk_decay = jnp.exp(log_gamma((1, RET_DV)) * blk)
    return decay, inner, zeta, blk_decay


def _ret_block(qb, kb, vb, gb, state, tables):
    decay, inner, zeta, blk_decay = tables
    scores = lax.dot_general(qb, kb, (((1,), (1,)), ((), ())),
                             preferred_element_type=F32) * decay
    intra = jnp.dot(scores.astype(BF16), vb, preferred_element_type=F32)
    cross = jnp.dot(qb, state.astype(BF16), preferred_element_type=F32) * inner
    kz = (kb.astype(F32) * zeta).T.astype(BF16)
    new_state = blk_decay * state + jnp.dot(kz, vb, preferred_element_type=F32)
    o = intra + cross
    mu = jnp.mean(o, axis=-1, keepdims=True)
    oc = o - mu
    var = jnp.mean(oc * oc, axis=-1, keepdims=True)
    on = oc * lax.rsqrt(var + GN_EPS)
    y = (jax.nn.silu(gb.astype(F32)) * on).astype(BF16)
    return y, new_state


def _ret_prompt_kernel(q_ref, k_ref, v_ref, g_ref, y_ref, s_ref, *, blk, n_blk):
    tables = _decay_tables(pl.program_id(1), blk)
    state = jnp.zeros((RET_DK, RET_DV), F32)
    for t in range(n_blk):
        rows = slice(t * blk, (t + 1) * blk)
        y, state = _ret_block(q_ref[rows, :], k_ref[rows, :], v_ref[rows, :],
                              g_ref[rows, :], state, tables)
        y_ref[rows, :] = y
    s_ref[0, 0] = state


def _ret_prompt(proj, batch, seq):
    blk = min(seq, RET_CHUNK)
    k_off = RET_QK // RET_DK
    v_off = 2 * RET_QK // RET_DV
    g_off = v_off + RET_V // RET_DV
    return pl.pallas_call(
        functools.partial(_ret_prompt_kernel, blk=blk, n_blk=seq // blk),
        grid=(batch, RET_HEADS),
        in_specs=[pl.BlockSpec((seq, RET_DK), lambda b, h: (b, h)),
                  pl.BlockSpec((seq, RET_DK), lambda b, h: (b, k_off + h)),
                  pl.BlockSpec((seq, RET_DV), lambda b, h: (b, v_off + h)),
                  pl.BlockSpec((seq, RET_DV), lambda b, h: (b, g_off + h))],
        out_specs=[pl.BlockSpec((seq, RET_DV), lambda b, h: (b, h)),
                   pl.BlockSpec((1, 1, RET_DK, RET_DV), lambda b, h: (b, h, 0, 0))],
        out_shape=[jax.ShapeDtypeStruct((proj.shape[0], RET_V), BF16),
                   jax.ShapeDtypeStruct((batch, RET_HEADS, RET_DK, RET_DV), F32)],
        compiler_params=_params(("parallel", "parallel"), 40),
        name="ret_prompt",
    )(proj, proj, proj, proj)


def _ret_sample_kernel(q_ref, k_ref, v_ref, g_ref, s0_ref, y_in_ref, y_ref, s_ref, *, blk):
    del y_in_ref
    for h in range(RET_HEADS):
        ck = slice(h * RET_DK, (h + 1) * RET_DK)
        cv = slice(h * RET_DV, (h + 1) * RET_DV)
        y, new_state = _ret_block(q_ref[:, ck], k_ref[:, ck], v_ref[:, cv], g_ref[:, cv],
                                  s0_ref[0, 0, h], _decay_tables(h, blk))
        y_ref[:, cv] = y
        s_ref[0, h] = new_state


def _ret_sample(proj, state0, y, n_prompt_rows, dec_batch, dec_seq):
    assert dec_seq <= RET_BLOCK and n_prompt_rows % dec_seq == 0
    r_off = n_prompt_rows // dec_seq
    state_shape = (RET_HEADS, RET_DK, RET_DV)
    return pl.pallas_call(
        functools.partial(_ret_sample_kernel, blk=dec_seq),
        grid=(dec_batch,),
        in_specs=[pl.BlockSpec((dec_seq, RET_QK), lambda s: (r_off + s, 0)),
                  pl.BlockSpec((dec_seq, RET_QK), lambda s: (r_off + s, 1)),
                  pl.BlockSpec((dec_seq, RET_V), lambda s: (r_off + s, 2 * RET_QK // RET_V)),
                  pl.BlockSpec((dec_seq, RET_V), lambda s: (r_off + s, 2 * RET_QK // RET_V + 1)),
                  pl.BlockSpec((1, 1) + state_shape, lambda s: (0, s, 0, 0, 0)),
                  ANY_SPEC],
        out_specs=[pl.BlockSpec((dec_seq, RET_V), lambda s: (r_off + s, 0)),
                   pl.BlockSpec((1,) + state_shape, lambda s: (s, 0, 0, 0))],
        out_shape=[jax.ShapeDtypeStruct(y.shape, BF16),
                   jax.ShapeDtypeStruct((dec_batch,) + state_shape, F32)],
        input_output_aliases={5: 0},
        compiler_params=_params(("arbitrary",), 40),
        name="ret_sample",
    )(proj, proj, proj, proj, state0, y)


def kernel(x_prompt, x_sample, state_ret, w_a_in, a_ln_g, a_ln_b, a_ws, a_bs, w_a_out,
           w_b_in, w_b_out, w_ffn_in, w_ffn_out, ln_mix_g, ln_mix_b, ln_ffn_g, ln_ffn_b):
    batch, seq, d = x_prompt.shape
    dec_batch, dec_seq, _ = x_sample.shape
    n_p = batch * seq
    n_s = dec_batch * dec_seq
    m = n_p + n_s
    assert d == D_MODEL and seq % GMLP_CHUNK == 0 and GMLP_CHUNK % dec_seq == 0
    assert w_a_in.shape[0] == 1 and w_b_in.shape[0] == 1 and state_ret.shape[0] == 1
    rows3 = lambda a: a.reshape(a.shape[0], 1, a.shape[1])
    a_ln_g, a_ln_b, ln_mix_g, ln_mix_b, ln_ffn_g, ln_ffn_b = map(
        rows3, (a_ln_g, a_ln_b, ln_mix_g, ln_mix_b, ln_ffn_g, ln_ffn_b))

    reps = GMLP_CHUNK // dec_seq
    expand = lambda b: jnp.repeat(b.T, GMLP_GROUP_DIM, axis=1)
    gmlp = functools.partial(_gmlp_layer, w_in=w_a_in, w_out=w_a_out, ln_g=a_ln_g, ln_b=a_ln_b,
                             mix_g=ln_mix_g, mix_b=ln_mix_b, m_total=m, dec_seq=dec_seq)
    x1, x1b = gmlp(x_prompt.reshape(n_p, d), wmix=a_ws[0], bias=expand(a_bs[0]), row_off=0)
    x1, x1b, v_sample = gmlp(
        x_sample.reshape(n_s, d), wmix=jnp.tile(a_ws[0][:, :dec_seq, :dec_seq], (1, reps, reps)),
        bias=expand(jnp.tile(a_bs[0][:, :dec_seq], (1, reps))), row_off=n_p, carry=(x1, x1b))
    act = _ffn_in(x1b, w_ffn_in, 0)
    x2, x2b = _proj_ln(act, w_ffn_out, 0, x1, ln_ffn_g, ln_ffn_b, 0, chunk_rows=512)

    cos, sin = _rope_tables(m, n_p, seq, dec_seq)
    proj = _ret_in(x2b, w_b_in, cos, sin)
    y, state_p = _ret_prompt(proj, batch, seq)
    y, state_s = _ret_sample(proj, state_ret, y, n_p, dec_batch, dec_seq)
    x3, x3b = _proj_ln(y, w_b_out, 0, x2, ln_mix_g, ln_mix_b, 1, chunk_rows=512)
    act = _ffn_in(x3b, w_ffn_in, 1)
    y_prompt, y_sample = _proj_ln(act, w_ffn_out, 1, x3, ln_ffn_g, ln_ffn_b, 1, chunk_rows=512,
                                  split_rows=n_p)

    return (y_prompt.reshape(batch, seq, d), y_sample.reshape(dec_batch, dec_seq, d),
            state_p[None], state_s[None], v_sample.reshape(1, dec_batch, dec_seq, d))
```

```python
import functools
import math

import jax
import jax.numpy as jnp
import numpy as np
from jax import lax
from jax.experimental import pallas as pl
from jax.experimental.pallas import tpu as pltpu

D_MODEL = 2048
DEPTH = 2
PAST_LEN = 4096
RET_BLOCK = 64
RET_CHUNK = 256
GMLP_CHUNK = 128
GMLP_GROUPS = 8
GMLP_GROUP_DIM = D_MODEL // GMLP_GROUPS
RET_HEADS = 8
RET_DK = D_MODEL // RET_HEADS
RET_DV = 2 * RET_DK
RET_QK = RET_HEADS * RET_DK
RET_V = RET_HEADS * RET_DV
ROPE_BASE = 10000.0
ROPE_HALF = RET_DK // 2
ALPHA = (2 * DEPTH) ** 0.25
LN_EPS = 1e-5
GN_EPS = 1e-6

F32 = jnp.float32
BF16 = jnp.bfloat16
MIB = 1024 * 1024
ANY_SPEC = pl.BlockSpec(memory_space=pl.ANY)


def _params(semantics, vmem_mib):
    return pltpu.CompilerParams(dimension_semantics=semantics,
                                vmem_limit_bytes=vmem_mib * MIB)


def _layer_norm(x, g, b):
    mu = jnp.mean(x, axis=-1, keepdims=True)
    xc = x - mu
    var = jnp.mean(xc * xc, axis=-1, keepdims=True)
    return xc * lax.rsqrt(var + LN_EPS) * g + b


def _gelu(x):
    return 0.5 * x * (1.0 + lax.erf(x * np.float32(math.sqrt(0.5))))


def _stage_weight(step, w_ref, wb_ref, n_chunks):
    rows = w_ref.shape[0]

    @pl.when(step < n_chunks)
    def _():
        r0 = pl.multiple_of(step * rows, rows)
        wb_ref[pl.ds(r0, rows), :] = w_ref[...].astype(BF16)


def _row_spec(d, layer):
    return pl.BlockSpec((None, 1, d), lambda s: (layer, 0, 0))


def _chunk_spec(layer, rows, cols, n_chunks):
    return pl.BlockSpec((None, rows, cols),
                        lambda s: (layer, jnp.minimum(s, n_chunks - 1), 0))


def _gmlp_kernel(*refs, tm, n_stage, n_tiles, is_sample, dec_seq):
    if is_sample:
        (x_ref, win_ref, wout_ref, lng_ref, lnb_ref, wmix_ref, bias_ref, g_ref, b_ref, _, _,
         x1_ref, x1b_ref, vn_ref, *scratch) = refs
    else:
        (x_ref, win_ref, wout_ref, lng_ref, lnb_ref, wmix_ref, bias_ref, g_ref, b_ref,
         x1_ref, x1b_ref, *scratch) = refs
    win_b, wout_b, u_ref, v_ref, s0, s1, xr0, xr1 = scratch
    step = pl.program_id(0)
    _stage_weight(step, win_ref, win_b, n_stage)
    _stage_weight(step, wout_ref, wout_b, n_stage)
    t = step - n_stage
    d = D_MODEL

    def gate_in(xr_ref):
        xf = x_ref[...]
        xr_ref[...] = xf
        xb = xf.astype(BF16)
        u_ref[...] = _gelu(jnp.dot(xb, win_b[:, :d], preferred_element_type=F32))
        v = _layer_norm(_gelu(jnp.dot(xb, win_b[:, d:], preferred_element_type=F32)),
                        lng_ref[...], lnb_ref[...])
        if is_sample:
            vn_ref[...] = v
        v_ref[...] = v.astype(BF16)

    def gate_mix(s_ref):
        row = lax.broadcasted_iota(jnp.int32, (GMLP_CHUNK, GMLP_CHUNK), 0)
        col = lax.broadcasted_iota(jnp.int32, (GMLP_CHUNK, GMLP_CHUNK), 1)
        mask = col <= row
        if is_sample:
            mask = mask & (((row ^ col) & ~(dec_seq - 1)) == 0)
        for g in range(GMLP_GROUPS):
            w_g = jnp.where(mask, wmix_ref[g], 0.0).astype(BF16)
            cs = slice(g * GMLP_GROUP_DIM, (g + 1) * GMLP_GROUP_DIM)
            for seg in range(tm // GMLP_CHUNK):
                rs = slice(seg * GMLP_CHUNK, (seg + 1) * GMLP_CHUNK)
                mixed = jnp.dot(w_g, v_ref[rs, cs], preferred_element_type=F32)
                mixed = mixed + bias_ref[:, cs]
                s_ref[rs, cs] = (u_ref[rs, cs] * mixed).astype(BF16)

    def project(s_ref):
        return jnp.dot(s_ref[...], wout_b[...], preferred_element_type=F32)

    def finish(h, xr_ref):
        x1 = _layer_norm(ALPHA * xr_ref[...] + h, g_ref[...], b_ref[...])
        x1_ref[...] = x1
        x1b_ref[...] = x1.astype(BF16)

    @pl.when(t == 0)
    def _():
        gate_in(xr0)
        gate_mix(s0)

    for parity, (cur, prev) in enumerate((((s0, xr0), (s1, xr1)), ((s1, xr1), (s0, xr0)))):
        @pl.when(((t & 1) == parity) & (t >= 1) & (t < n_tiles))
        def _(cur=cur, prev=prev):
            gate_in(cur[1])
            h = project(prev[0])
            gate_mix(cur[0])
            finish(h, prev[1])

    @pl.when(t == n_tiles)
    def _():
        last = (s0, xr0) if (n_tiles - 1) % 2 == 0 else (s1, xr1)
        finish(project(last[0]), last[1])


def _gmlp_layer(x, w_in, w_out, ln_g, ln_b, wmix, bias, mix_g, mix_b, m_total, row_off,
                dec_seq, carry=None, tm=256, n_stage=16):
    rows, d = x.shape
    is_sample = carry is not None
    n_tiles = rows // tm
    off = row_off // tm
    gate_tile = lambda s: (jnp.clip(s - n_stage, 0, n_tiles - 1), 0)
    out_tile = lambda s: (off + jnp.clip(s - n_stage - 1, 0, n_tiles - 1), 0)
    row0 = _row_spec(d, 0)
    in_specs = [pl.BlockSpec((tm, d), gate_tile),
                _chunk_spec(0, d // n_stage, 2 * d, n_stage),
                _chunk_spec(0, d // n_stage, d, n_stage),
                row0, row0,
                pl.BlockSpec((GMLP_GROUPS, GMLP_CHUNK, GMLP_CHUNK), lambda s: (0, 0, 0)),
                pl.BlockSpec((GMLP_CHUNK, d), lambda s: (0, 0)),
                row0, row0]
    out_specs = [pl.BlockSpec((tm, d), out_tile), pl.BlockSpec((tm, d), out_tile)]
    out_shape = [jax.ShapeDtypeStruct((m_total, d), F32),
                 jax.ShapeDtypeStruct((m_total, d), BF16)]
    args = [x, w_in, w_out, ln_g, ln_b, wmix, bias, mix_g, mix_b]
    aliases = {}
    if is_sample:
        in_specs += [ANY_SPEC, ANY_SPEC]
        args += list(carry)
        aliases = {len(args) - 2: 0, len(args) - 1: 1}
        out_specs.append(pl.BlockSpec((tm, d), gate_tile))
        out_shape.append(jax.ShapeDtypeStruct((rows, d), F32))
    return pl.pallas_call(
        functools.partial(_gmlp_kernel, tm=tm, n_stage=n_stage, n_tiles=n_tiles,
                          is_sample=is_sample, dec_seq=dec_seq),
        grid=(n_stage + n_tiles + 1,),
        in_specs=in_specs,
        out_specs=out_specs,
        out_shape=out_shape,
        scratch_shapes=[pltpu.VMEM((d, 2 * d), BF16), pltpu.VMEM((d, d), BF16),
                        pltpu.VMEM((tm, d), F32), pltpu.VMEM((tm, d), BF16),
                        pltpu.VMEM((tm, d), BF16), pltpu.VMEM((tm, d), BF16),
                        pltpu.VMEM((tm, d), F32), pltpu.VMEM((tm, d), F32)],
        input_output_aliases=aliases,
        compiler_params=_params(("arbitrary",), 60),
        name="gmlp_sample" if is_sample else "gmlp_prompt",
    )(*args)


def _ffn_in_kernel(x_ref, wg_ref, wu_ref, o_ref, wg_b, wu_b):
    @pl.when(pl.program_id(1) == 0)
    def _():
        wg_b[...] = wg_ref[...].astype(BF16)
        wu_b[...] = wu_ref[...].astype(BF16)

    xb = x_ref[...]
    gate = jnp.dot(xb, wg_b[...], preferred_element_type=F32)
    up = jnp.dot(xb, wu_b[...], preferred_element_type=F32)
    o_ref[...] = (jax.nn.silu(gate) * up).astype(BF16)


def _ffn_in(xb, w_in, layer, tm=1152, th=512):
    m, d = xb.shape
    hidden = w_in.shape[2] // 2
    nh = hidden // th
    return pl.pallas_call(
        _ffn_in_kernel,
        grid=(nh, m // tm),
        in_specs=[pl.BlockSpec((tm, d), lambda j, i: (i, 0)),
                  pl.BlockSpec((None, d, th), lambda j, i: (layer, 0, j)),
                  pl.BlockSpec((None, d, th), lambda j, i: (layer, 0, j + nh))],
        out_specs=pl.BlockSpec((tm, th), lambda j, i: (i, j)),
        out_shape=jax.ShapeDtypeStruct((m, hidden), BF16),
        scratch_shapes=[pltpu.VMEM((d, th), BF16), pltpu.VMEM((d, th), BF16)],
        compiler_params=_params(("arbitrary", "arbitrary"), 48),
        name="ffn_in",
    )(xb, w_in, w_in)


def _proj_ln_kernel(a_ref, w_ref, x_ref, g_ref, b_ref, *rest, n_stage, n_tiles, n_prompt_tiles):
    outs, (w_b, h0, h1) = rest[:-3], rest[-3:]
    step = pl.program_id(0)
    _stage_weight(step, w_ref, w_b, n_stage)
    t = step - n_stage
    split = n_prompt_tiles is not None

    def matmul(dst):
        dst[...] = jnp.dot(a_ref[...], w_b[...], preferred_element_type=F32)

    def epilogue(src, sample):
        x = _layer_norm(ALPHA * x_ref[...] + src[...], g_ref[...], b_ref[...])
        if split:
            outs[1 if sample else 0][...] = x
        else:
            outs[0][...] = x
            outs[1][...] = x.astype(BF16)

    @pl.when(t == 0)
    def _():
        matmul(h0)

    for parity, (dst, src) in enumerate(((h0, h1), (h1, h0))):
        for sample in ((False, True) if split else (False,)):
            cond = ((t & 1) == parity) & (t >= 1) & (t < n_tiles)
            if split:
                cond &= (t - 1 >= n_prompt_tiles) if sample else (t - 1 < n_prompt_tiles)

            @pl.when(cond)
            def _(dst=dst, src=src, sample=sample):
                matmul(dst)
                epilogue(src, sample)

    @pl.when(t == n_tiles)
    def _():
        epilogue(h0 if (n_tiles - 1) % 2 == 0 else h1, split)


def _proj_ln(a, w, layer, x, ln_g, ln_b, ln_row, chunk_rows, split_rows=None, tm=256):
    m, kdim = a.shape
    d = w.shape[2]
    n_stage = kdim // chunk_rows
    n_tiles = m // tm
    mm_tile = lambda s: (jnp.clip(s - n_stage, 0, n_tiles - 1), 0)
    ln_tile = lambda s: (jnp.clip(s - n_stage - 1, 0, n_tiles - 1), 0)
    in_specs = [pl.BlockSpec((tm, kdim), mm_tile),
                _chunk_spec(layer, chunk_rows, d, n_stage),
                pl.BlockSpec((tm, d), ln_tile),
                _row_spec(d, ln_row), _row_spec(d, ln_row)]
    if split_rows is None:
        n_prompt_tiles = None
        out_specs = [pl.BlockSpec((tm, d), ln_tile), pl.BlockSpec((tm, d), ln_tile)]
        out_shape = [jax.ShapeDtypeStruct((m, d), F32), jax.ShapeDtypeStruct((m, d), BF16)]
    else:
        n_prompt_tiles = split_rows // tm
        n_sample_tiles = n_tiles - n_prompt_tiles
        assert n_sample_tiles >= 1
        out_specs = [
            pl.BlockSpec((tm, d), lambda s: (jnp.clip(s - n_stage - 1, 0, n_prompt_tiles - 1), 0)),
            pl.BlockSpec((tm, d), lambda s: (
                jnp.clip(s - n_stage - 1 - n_prompt_tiles, 0, n_sample_tiles - 1), 0))]
        out_shape = [jax.ShapeDtypeStruct((split_rows, d), F32),
                     jax.ShapeDtypeStruct((m - split_rows, d), F32)]
    return pl.pallas_call(
        functools.partial(_proj_ln_kernel, n_stage=n_stage, n_tiles=n_tiles,
                          n_prompt_tiles=n_prompt_tiles),
        grid=(n_stage + n_tiles + 1,),
        in_specs=in_specs,
        out_specs=out_specs,
        out_shape=out_shape,
        scratch_shapes=[pltpu.VMEM((kdim, d), BF16), pltpu.VMEM((tm, d), F32),
                        pltpu.VMEM((tm, d), F32)],
        compiler_params=_params(("arbitrary",), 60),
        name="proj_ln_split" if split_rows else "proj_ln",
    )(a, w, x, ln_g, ln_b)


def _rope_table_kernel(cos_ref, sin_ref, *, rows, seq, dec_seq):
    r = lax.broadcasted_iota(jnp.int32, (rows, ROPE_HALF), 0) + pl.program_id(0) * rows
    j = lax.broadcasted_iota(jnp.int32, (rows, ROPE_HALF), 1)
    pos = jnp.where(r < seq, r, PAST_LEN + ((r - seq) & (dec_seq - 1)))
    inv = jnp.power(np.float32(ROPE_BASE), -j.astype(F32) / ROPE_HALF)
    ang = pos.astype(F32) * inv
    cos_ref[...] = jnp.cos(ang)
    sin_ref[...] = jnp.sin(ang)


def _rope_tables(seq, dec_seq, rows):
    assert seq % rows == 0 and rows % dec_seq == 0 and dec_seq & (dec_seq - 1) == 0
    spec = pl.BlockSpec((rows, ROPE_HALF), lambda i: (i, 0))
    return pl.pallas_call(
        functools.partial(_rope_table_kernel, rows=rows, seq=seq, dec_seq=dec_seq),
        grid=(seq // rows + 1,),
        out_specs=[spec, spec],
        out_shape=[jax.ShapeDtypeStruct((seq + rows, ROPE_HALF), F32)] * 2,
        compiler_params=_params(("parallel",), 16),
        name="rope_tables",
    )()


def _ret_in_kernel(x_ref, w_ref, *rest, tn, rope):
    if rope:
        cos_ref, sin_ref, o_ref, w_b = rest
    else:
        o_ref, w_b = rest

    @pl.when(pl.program_id(1) == 0)
    def _():
        w_b[...] = w_ref[...].astype(BF16)

    acc = jnp.dot(x_ref[...], w_b[...], preferred_element_type=F32)
    if not rope:
        o_ref[...] = acc.astype(BF16)
        return
    scale = jnp.where(pl.program_id(0) < RET_QK // tn, np.float32(RET_DK ** -0.5), np.float32(1.0))
    cos = cos_ref[...]
    sin = sin_ref[...]
    for hh in range(tn // RET_DK):
        a = acc[:, hh * RET_DK:hh * RET_DK + ROPE_HALF]
        b = acc[:, hh * RET_DK + ROPE_HALF:(hh + 1) * RET_DK]
        o_ref[:, hh * RET_DK:hh * RET_DK + ROPE_HALF] = ((a * cos - b * sin) * scale).astype(BF16)
        o_ref[:, hh * RET_DK + ROPE_HALF:(hh + 1) * RET_DK] = ((a * sin + b * cos) * scale).astype(BF16)


def _ret_in(xb, w, col0, n_cols, tm, rope=None, tn=1024):
    m, d = xb.shape
    in_specs = [pl.BlockSpec((tm, d), lambda j, i: (i, 0)),
                pl.BlockSpec((None, d, tn), lambda j, i: (0, 0, col0 // tn + j))]
    args = [xb, w]
    if rope:
        cos, sin, n_prompt_rows, seq = rope
        assert cos.shape[0] == seq + tm and n_prompt_rows % tm == 0
        per_seq = seq // tm
        table_tile = lambda j, i: (jnp.where(i < n_prompt_rows // tm, i % per_seq, per_seq), 0)
        in_specs += [pl.BlockSpec((tm, ROPE_HALF), table_tile)] * 2
        args += [cos, sin]
    return pl.pallas_call(
        functools.partial(_ret_in_kernel, tn=tn, rope=bool(rope)),
        grid=(n_cols // tn, m // tm),
        in_specs=in_specs,
        out_specs=pl.BlockSpec((tm, tn), lambda j, i: (i, j)),
        out_shape=jax.ShapeDtypeStruct((m, n_cols), BF16),
        scratch_shapes=[pltpu.VMEM((d, tn), BF16)],
        compiler_params=_params(("arbitrary", "arbitrary"), 48),
        name="ret_in_qk" if rope else "ret_in_vg",
    )(*args)


def _decay_tables(head, blk):
    def log_gamma(shape):
        hv = jnp.full(shape, head, dtype=jnp.int32).astype(F32)
        return jnp.log1p(-jnp.exp2(-5.0 - hv))

    row = lax.broadcasted_iota(jnp.int32, (blk, blk), 0).astype(F32)
    col = lax.broadcasted_iota(jnp.int32, (blk, blk), 1).astype(F32)
    diff = row - col
    decay = jnp.where(diff >= 0, jnp.exp(log_gamma((blk, blk)) * jnp.maximum(diff, 0.0)), 0.0)
    idx_v = lax.broadcasted_iota(jnp.int32, (blk, RET_DV), 0).astype(F32)
    idx_k = lax.broadcasted_iota(jnp.int32, (blk, RET_DK), 0).astype(F32)
    inner = jnp.exp(log_gamma((blk, RET_DV)) * (idx_v + 1.0))
    zeta = jnp.exp(log_gamma((blk, RET_DK)) * (blk - 1.0 - idx_k))
    blk_decay = jnp.exp(log_gamma((1, RET_DV)) * blk)
    return decay, inner, zeta, blk_decay


def _ret_block(qb, kb, vb, gb, state, tables):
    decay, inner, zeta, blk_decay = tables
    scores = lax.dot_general(qb, kb, (((1,), (1,)), ((), ())),
                             preferred_element_type=F32) * decay
    intra = jnp.dot(scores.astype(BF16), vb, preferred_element_type=F32)
    cross = jnp.dot(qb, state.astype(BF16), preferred_element_type=F32) * inner
    kz = (kb.astype(F32) * zeta).T.astype(BF16)
    new_state = blk_decay * state + jnp.dot(kz, vb, preferred_element_type=F32)
    o = intra + cross
    mu = jnp.mean(o, axis=-1, keepdims=True)
    oc = o - mu
    var = jnp.mean(oc * oc, axis=-1, keepdims=True)
    on = oc * lax.rsqrt(var + GN_EPS)
    y = (jax.nn.silu(gb.astype(F32)) * on).astype(BF16)
    return y, new_state


def _ret_prompt_kernel(q_ref, k_ref, v_ref, g_ref, y_ref, s_ref, *, blk, n_blk):
    tables = _decay_tables(pl.program_id(1), blk)
    state = jnp.zeros((RET_DK, RET_DV), F32)
    for t in range(n_blk):
        rows = slice(t * blk, (t + 1) * blk)
        y, state = _ret_block(q_ref[rows, :], k_ref[rows, :], v_ref[rows, :],
                              g_ref[rows, :], state, tables)
        y_ref[rows, :] = y
    s_ref[0, 0] = state


def _ret_prompt(qk, vg, batch, seq):
    blk = min(seq, RET_CHUNK)
    return pl.pallas_call(
        functools.partial(_ret_prompt_kernel, blk=blk, n_blk=seq // blk),
        grid=(batch, RET_HEADS),
        in_specs=[pl.BlockSpec((seq, RET_DK), lambda b, h: (b, h)),
                  pl.BlockSpec((seq, RET_DK), lambda b, h: (b, RET_HEADS + h)),
                  pl.BlockSpec((seq, RET_DV), lambda b, h: (b, h)),
                  pl.BlockSpec((seq, RET_DV), lambda b, h: (b, RET_HEADS + h))],
        out_specs=[pl.BlockSpec((seq, RET_DV), lambda b, h: (b, h)),
                   pl.BlockSpec((1, 1, RET_DK, RET_DV), lambda b, h: (b, h, 0, 0))],
        out_shape=[jax.ShapeDtypeStruct((qk.shape[0], RET_V), BF16),
                   jax.ShapeDtypeStruct((batch, RET_HEADS, RET_DK, RET_DV), F32)],
        compiler_params=_params(("parallel", "parallel"), 40),
        name="ret_prompt",
    )(qk, qk, vg, vg)


def _ret_sample_kernel(q_ref, k_ref, v_ref, g_ref, s0_ref, y_in_ref, y_ref, s_ref, *, blk):
    del y_in_ref
    for h in range(RET_HEADS):
        ck = slice(h * RET_DK, (h + 1) * RET_DK)
        cv = slice(h * RET_DV, (h + 1) * RET_DV)
        y, new_state = _ret_block(q_ref[:, ck], k_ref[:, ck], v_ref[:, cv], g_ref[:, cv],
                                  s0_ref[0, 0, h], _decay_tables(h, blk))
        y_ref[:, cv] = y
        s_ref[0, h] = new_state


def _ret_sample(qk, vg, state0, y, n_prompt_rows, dec_batch, dec_seq):
    assert dec_seq <= RET_BLOCK and n_prompt_rows % dec_seq == 0
    r_off = n_prompt_rows // dec_seq
    state_shape = (RET_HEADS, RET_DK, RET_DV)
    return pl.pallas_call(
        functools.partial(_ret_sample_kernel, blk=dec_seq),
        grid=(dec_batch,),
        in_specs=[pl.BlockSpec((dec_seq, RET_QK), lambda s: (r_off + s, 0)),
                  pl.BlockSpec((dec_seq, RET_QK), lambda s: (r_off + s, 1)),
                  pl.BlockSpec((dec_seq, RET_V), lambda s: (r_off + s, 0)),
                  pl.BlockSpec((dec_seq, RET_V), lambda s: (r_off + s, 1)),
                  pl.BlockSpec((1, 1) + state_shape, lambda s: (0, s, 0, 0, 0)),
                  ANY_SPEC],
        out_specs=[pl.BlockSpec((dec_seq, RET_V), lambda s: (r_off + s, 0)),
                   pl.BlockSpec((1,) + state_shape, lambda s: (s, 0, 0, 0))],
        out_shape=[jax.ShapeDtypeStruct(y.shape, BF16),
                   jax.ShapeDtypeStruct((dec_batch,) + state_shape, F32)],
        input_output_aliases={5: 0},
        compiler_params=_params(("arbitrary",), 40),
        name="ret_sample",
    )(qk, qk, vg, vg, state0, y)


def kernel(x_prompt, x_sample, state_ret, w_a_in, a_ln_g, a_ln_b, a_ws, a_bs, w_a_out,
           w_b_in, w_b_out, w_ffn_in, w_ffn_out, ln_mix_g, ln_mix_b, ln_ffn_g, ln_ffn_b):
    batch, seq, d = x_prompt.shape
    dec_batch, dec_seq, _ = x_sample.shape
    n_p = batch * seq
    n_s = dec_batch * dec_seq
    m = n_p + n_s
    assert d == D_MODEL and seq % GMLP_CHUNK == 0 and GMLP_CHUNK % dec_seq == 0
    assert w_a_in.shape[0] == 1 and w_b_in.shape[0] == 1 and state_ret.shape[0] == 1
    rows3 = lambda a: a.reshape(a.shape[0], 1, a.shape[1])
    a_ln_g, a_ln_b, ln_mix_g, ln_mix_b, ln_ffn_g, ln_ffn_b = map(
        rows3, (a_ln_g, a_ln_b, ln_mix_g, ln_mix_b, ln_ffn_g, ln_ffn_b))

    reps = GMLP_CHUNK // dec_seq
    expand = lambda b: jnp.repeat(b.T, GMLP_GROUP_DIM, axis=1)
    gmlp = functools.partial(_gmlp_layer, w_in=w_a_in, w_out=w_a_out, ln_g=a_ln_g, ln_b=a_ln_b,
                             mix_g=ln_mix_g, mix_b=ln_mix_b, m_total=m, dec_seq=dec_seq)
    x1, x1b = gmlp(x_prompt.reshape(n_p, d), wmix=a_ws[0], bias=expand(a_bs[0]), row_off=0)
    x1, x1b, v_sample = gmlp(
        x_sample.reshape(n_s, d), wmix=jnp.tile(a_ws[0][:, :dec_seq, :dec_seq], (1, reps, reps)),
        bias=expand(jnp.tile(a_bs[0][:, :dec_seq], (1, reps))), row_off=n_p, carry=(x1, x1b))
    act = _ffn_in(x1b, w_ffn_in, 0)
    x2, x2b = _proj_ln(act, w_ffn_out, 0, x1, ln_ffn_g, ln_ffn_b, 0, chunk_rows=512)

    cos, sin = _rope_tables(seq, dec_seq, rows=1024)
    qk = _ret_in(x2b, w_b_in, 0, 2 * RET_QK, tm=1024, rope=(cos, sin, n_p, seq))
    vg = _ret_in(x2b, w_b_in, 2 * RET_QK, 2 * RET_V, tm=1152)
    y, state_p = _ret_prompt(qk, vg, batch, seq)
    y, state_s = _ret_sample(qk, vg, state_ret, y, n_p, dec_batch, dec_seq)
    x3, x3b = _proj_ln(y, w_b_out, 0, x2, ln_mix_g, ln_mix_b, 1, chunk_rows=512)
    act = _ffn_in(x3b, w_ffn_in, 1)
    y_prompt, y_sample = _proj_ln(act, w_ffn_out, 1, x3, ln_ffn_g, ln_ffn_b, 1, chunk_rows=512,
                                  split_rows=n_p)

    return (y_prompt.reshape(batch, seq, d), y_sample.reshape(dec_batch, dec_seq, d),
            state_p[None], state_s[None], v_sample.reshape(1, dec_batch, dec_seq, d))
```

```python
import functools
import math

import jax
import jax.numpy as jnp
import numpy as np
from jax import lax
from jax.experimental import pallas as pl
from jax.experimental.pallas import tpu as pltpu

D_MODEL = 2048
DEPTH = 2
PAST_LEN = 4096
RET_BLOCK = 64
RET_CHUNK = 256
GMLP_CHUNK = 128
GMLP_GROUPS = 8
GMLP_GROUP_DIM = D_MODEL // GMLP_GROUPS
RET_HEADS = 8
RET_DK = D_MODEL // RET_HEADS
RET_DV = 2 * RET_DK
RET_QK = RET_HEADS * RET_DK
RET_V = RET_HEADS * RET_DV
ROPE_BASE = 10000.0
ROPE_HALF = RET_DK // 2
ALPHA = (2 * DEPTH) ** 0.25
LN_EPS = 1e-5
GN_EPS = 1e-6

F32 = jnp.float32
BF16 = jnp.bfloat16
MIB = 1024 * 1024
ANY_SPEC = pl.BlockSpec(memory_space=pl.ANY)


def _params(semantics, vmem_mib):
    return pltpu.CompilerParams(dimension_semantics=semantics,
                                vmem_limit_bytes=vmem_mib * MIB)


def _layer_norm(x, g, b):
    mu = jnp.mean(x, axis=-1, keepdims=True)
    xc = x - mu
    var = jnp.mean(xc * xc, axis=-1, keepdims=True)
    return xc * lax.rsqrt(var + LN_EPS) * g + b


def _gelu(x):
    return 0.5 * x * (1.0 + lax.erf(x * np.float32(math.sqrt(0.5))))


def _stage_weight(step, w_ref, wb_ref, n_chunks):
    rows = w_ref.shape[0]

    @pl.when(step < n_chunks)
    def _():
        r0 = pl.multiple_of(step * rows, rows)
        wb_ref[pl.ds(r0, rows), :] = w_ref[...].astype(BF16)


def _row_spec(d, layer):
    return pl.BlockSpec((None, 1, d), lambda s: (layer, 0, 0))


def _chunk_spec(layer, rows, cols, n_chunks):
    return pl.BlockSpec((None, rows, cols),
                        lambda s: (layer, jnp.minimum(s, n_chunks - 1), 0))


def _gmlp_kernel(*refs, tm, n_stage, n_tiles, n_fill, is_sample, dec_seq):
    if is_sample:
        (x_ref, win_ref, wout_ref, lng_ref, lnb_ref, wmix_ref, bias_ref, g_ref, b_ref, _, _,
         x1_ref, x1b_ref, vn_ref, *scratch) = refs
    else:
        (x_ref, win_ref, wout_ref, lng_ref, lnb_ref, wmix_ref, bias_ref, g_ref, b_ref,
         x1_ref, x1b_ref, *scratch) = refs
    win_b, wout_b, u_ref, v_ref, s0, s1, xr0, xr1 = scratch
    step = pl.program_id(0)
    _stage_weight(step, win_ref, win_b, n_stage)
    _stage_weight(step, wout_ref, wout_b, n_stage)
    t = step - n_stage
    d = D_MODEL

    @pl.when(step < n_fill)
    def _():
        x1_ref[...] = jnp.zeros_like(x1_ref)
        x1b_ref[...] = jnp.zeros_like(x1b_ref)

    def gate_in(xr_ref):
        xf = x_ref[...]
        xr_ref[...] = xf
        xb = xf.astype(BF16)
        u_ref[...] = _gelu(jnp.dot(xb, win_b[:, :d], preferred_element_type=F32))
        v = _layer_norm(_gelu(jnp.dot(xb, win_b[:, d:], preferred_element_type=F32)),
                        lng_ref[...], lnb_ref[...])
        if is_sample:
            vn_ref[...] = v
        v_ref[...] = v.astype(BF16)

    def gate_mix(s_ref):
        row = lax.broadcasted_iota(jnp.int32, (GMLP_CHUNK, GMLP_CHUNK), 0)
        col = lax.broadcasted_iota(jnp.int32, (GMLP_CHUNK, GMLP_CHUNK), 1)
        mask = col <= row
        if is_sample:
            mask = mask & (((row ^ col) & ~(dec_seq - 1)) == 0)
        for g in range(GMLP_GROUPS):
            w_g = jnp.where(mask, wmix_ref[g], 0.0).astype(BF16)
            cs = slice(g * GMLP_GROUP_DIM, (g + 1) * GMLP_GROUP_DIM)
            for seg in range(tm // GMLP_CHUNK):
                rs = slice(seg * GMLP_CHUNK, (seg + 1) * GMLP_CHUNK)
                mixed = jnp.dot(w_g, v_ref[rs, cs], preferred_element_type=F32)
                mixed = mixed + bias_ref[:, cs]
                s_ref[rs, cs] = (u_ref[rs, cs] * mixed).astype(BF16)

    def project(s_ref):
        return jnp.dot(s_ref[...], wout_b[...], preferred_element_type=F32)

    def finish(h, xr_ref):
        x1 = _layer_norm(ALPHA * xr_ref[...] + h, g_ref[...], b_ref[...])
        x1_ref[...] = x1
        x1b_ref[...] = x1.astype(BF16)

    @pl.when(t == 0)
    def _():
        gate_in(xr0)
        gate_mix(s0)

    for parity, (cur, prev) in enumerate((((s0, xr0), (s1, xr1)), ((s1, xr1), (s0, xr0)))):
        @pl.when(((t & 1) == parity) & (t >= 1) & (t < n_tiles))
        def _(cur=cur, prev=prev):
            gate_in(cur[1])
            h = project(prev[0])
            gate_mix(cur[0])
            finish(h, prev[1])

    @pl.when(t == n_tiles)
    def _():
        last = (s0, xr0) if (n_tiles - 1) % 2 == 0 else (s1, xr1)
        finish(project(last[0]), last[1])


def _gmlp_layer(x, w_in, w_out, ln_g, ln_b, wmix, bias, mix_g, mix_b, m_total, row_off,
                dec_seq, carry=None, tm=256, n_stage=16):
    rows, d = x.shape
    is_sample = carry is not None
    n_tiles = rows // tm
    off = row_off // tm
    n_fill = 0 if is_sample else (m_total - rows) // tm
    assert n_fill <= n_stage and (is_sample or row_off == 0)
    gate_tile = lambda s: (jnp.clip(s - n_stage, 0, n_tiles - 1), 0)
    out_tile = lambda s: (jnp.where(s < n_fill, n_tiles + s,
                                    off + jnp.clip(s - n_stage - 1, 0, n_tiles - 1)), 0)
    row0 = _row_spec(d, 0)
    in_specs = [pl.BlockSpec((tm, d), gate_tile),
                _chunk_spec(0, d // n_stage, 2 * d, n_stage),
                _chunk_spec(0, d // n_stage, d, n_stage),
                row0, row0,
                pl.BlockSpec((GMLP_GROUPS, GMLP_CHUNK, GMLP_CHUNK), lambda s: (0, 0, 0)),
                pl.BlockSpec((GMLP_CHUNK, d), lambda s: (0, 0)),
                row0, row0]
    out_specs = [pl.BlockSpec((tm, d), out_tile), pl.BlockSpec((tm, d), out_tile)]
    out_shape = [jax.ShapeDtypeStruct((m_total, d), F32),
                 jax.ShapeDtypeStruct((m_total, d), BF16)]
    args = [x, w_in, w_out, ln_g, ln_b, wmix, bias, mix_g, mix_b]
    aliases = {}
    if is_sample:
        in_specs += [ANY_SPEC, ANY_SPEC]
        args += list(carry)
        aliases = {len(args) - 2: 0, len(args) - 1: 1}
        out_specs.append(pl.BlockSpec((tm, d), gate_tile))
        out_shape.append(jax.ShapeDtypeStruct((rows, d), F32))
    return pl.pallas_call(
        functools.partial(_gmlp_kernel, tm=tm, n_stage=n_stage, n_tiles=n_tiles, n_fill=n_fill,
                          is_sample=is_sample, dec_seq=dec_seq),
        grid=(n_stage + n_tiles + 1,),
        in_specs=in_specs,
        out_specs=out_specs,
        out_shape=out_shape,
        scratch_shapes=[pltpu.VMEM((d, 2 * d), BF16), pltpu.VMEM((d, d), BF16),
                        pltpu.VMEM((tm, d), F32), pltpu.VMEM((tm, d), BF16),
                        pltpu.VMEM((tm, d), BF16), pltpu.VMEM((tm, d), BF16),
                        pltpu.VMEM((tm, d), F32), pltpu.VMEM((tm, d), F32)],
        input_output_aliases=aliases,
        compiler_params=_params(("arbitrary",), 60),
        name="gmlp_sample" if is_sample else "gmlp_prompt",
    )(*args)


def _ffn_in_kernel(x_ref, wg_ref, wu_ref, o_ref, wg_b, wu_b, *, parts):
    @pl.when(pl.program_id(1) == 0)
    def _():
        wg_b[...] = wg_ref[...].astype(BF16)
        wu_b[...] = wu_ref[...].astype(BF16)

    rows = x_ref.shape[0] // parts
    for p in range(parts):
        rs = slice(p * rows, (p + 1) * rows)
        xb = x_ref[rs, :]
        gate = jnp.dot(xb, wg_b[...], preferred_element_type=F32)
        up = jnp.dot(xb, wu_b[...], preferred_element_type=F32)
        o_ref[rs, :] = (jax.nn.silu(gate) * up).astype(BF16)


def _ffn_in(xb, w_in, layer, tm=2304, th=512, parts=2):
    m, d = xb.shape
    hidden = w_in.shape[2] // 2
    nh = hidden // th
    return pl.pallas_call(
        functools.partial(_ffn_in_kernel, parts=parts),
        grid=(nh, m // tm),
        in_specs=[pl.BlockSpec((tm, d), lambda j, i: (i, 0)),
                  pl.BlockSpec((None, d, th), lambda j, i: (layer, 0, j)),
                  pl.BlockSpec((None, d, th), lambda j, i: (layer, 0, j + nh))],
        out_specs=pl.BlockSpec((tm, th), lambda j, i: (i, j)),
        out_shape=jax.ShapeDtypeStruct((m, hidden), BF16),
        scratch_shapes=[pltpu.VMEM((d, th), BF16), pltpu.VMEM((d, th), BF16)],
        compiler_params=_params(("arbitrary", "arbitrary"), 60),
        name="ffn_in",
    )(xb, w_in, w_in)


def _proj_ln_kernel(*refs, n_stage, n_tiles, src_split, out_split):
    n_a = 1 if src_split is None else 2
    a_refs, (w_ref, x_ref, g_ref, b_ref), rest = refs[:n_a], refs[n_a:n_a + 4], refs[n_a + 4:]
    outs, (w_b, h0, h1) = rest[:-3], rest[-3:]
    step = pl.program_id(0)
    _stage_weight(step, w_ref, w_b, n_stage)
    t = step - n_stage

    def matmul(dst, second_src):
        a_ref = a_refs[1 if second_src else 0]
        dst[...] = jnp.dot(a_ref[...], w_b[...], preferred_element_type=F32)

    def epilogue(src, second_out):
        x = _layer_norm(ALPHA * x_ref[...] + src[...], g_ref[...], b_ref[...])
        if out_split is None:
            outs[0][...] = x
            outs[1][...] = x.astype(BF16)
        else:
            outs[1 if second_out else 0][...] = x

    @pl.when(t == 0)
    def _():
        matmul(h0, False)

    for parity, (dst, src) in enumerate(((h0, h1), (h1, h0))):
        for second_src in ((False, True) if src_split is not None else (False,)):
            for second_out in ((False, True) if out_split is not None else (False,)):
                cond = ((t & 1) == parity) & (t >= 1) & (t < n_tiles)
                if src_split is not None:
                    cond &= (t >= src_split) if second_src else (t < src_split)
                if out_split is not None:
                    cond &= (t - 1 >= out_split) if second_out else (t - 1 < out_split)

                @pl.when(cond)
                def _(dst=dst, src=src, second_src=second_src, second_out=second_out):
                    matmul(dst, second_src)
                    epilogue(src, second_out)

    @pl.when(t == n_tiles)
    def _():
        epilogue(h0 if (n_tiles - 1) % 2 == 0 else h1, out_split is not None)


def _proj_ln(a, w, layer, x, ln_g, ln_b, ln_row, chunk_rows, split_rows=None, tm=256):
    a_parts = a if isinstance(a, tuple) else (a,)
    kdim = a_parts[0].shape[1]
    m = sum(p.shape[0] for p in a_parts)
    d = w.shape[2]
    n_stage = kdim // chunk_rows
    n_tiles = m // tm
    clip = lambda v, n: jnp.clip(v, 0, n - 1)
    ln_tile = lambda s: (clip(s - n_stage - 1, n_tiles), 0)
    if len(a_parts) == 1:
        src_split = None
        a_specs = [pl.BlockSpec((tm, kdim), lambda s: (clip(s - n_stage, n_tiles), 0))]
    else:
        src_split = a_parts[0].shape[0] // tm
        assert 1 <= src_split < n_tiles and a_parts[0].shape[0] % tm == 0
        a_specs = [pl.BlockSpec((tm, kdim), lambda s: (clip(s - n_stage, src_split), 0)),
                   pl.BlockSpec((tm, kdim),
                                lambda s: (clip(s - n_stage - src_split, n_tiles - src_split), 0))]
    in_specs = a_specs + [_chunk_spec(layer, chunk_rows, d, n_stage),
                          pl.BlockSpec((tm, d), ln_tile),
                          _row_spec(d, ln_row), _row_spec(d, ln_row)]
    if split_rows is None:
        out_split = None
        out_specs = [pl.BlockSpec((tm, d), ln_tile), pl.BlockSpec((tm, d), ln_tile)]
        out_shape = [jax.ShapeDtypeStruct((m, d), F32), jax.ShapeDtypeStruct((m, d), BF16)]
    else:
        out_split = split_rows // tm
        assert 1 <= out_split < n_tiles and split_rows % tm == 0
        out_specs = [
            pl.BlockSpec((tm, d), lambda s: (clip(s - n_stage - 1, out_split), 0)),
            pl.BlockSpec((tm, d), lambda s: (clip(s - n_stage - 1 - out_split, n_tiles - out_split), 0))]
        out_shape = [jax.ShapeDtypeStruct((split_rows, d), F32),
                     jax.ShapeDtypeStruct((m - split_rows, d), F32)]
    return pl.pallas_call(
        functools.partial(_proj_ln_kernel, n_stage=n_stage, n_tiles=n_tiles,
                          src_split=src_split, out_split=out_split),
        grid=(n_stage + n_tiles + 1,),
        in_specs=in_specs,
        out_specs=out_specs,
        out_shape=out_shape,
        scratch_shapes=[pltpu.VMEM((kdim, d), BF16), pltpu.VMEM((tm, d), F32),
                        pltpu.VMEM((tm, d), F32)],
        compiler_params=_params(("arbitrary",), 60),
        name="proj_ln_split" if split_rows else "proj_ln",
    )(*a_parts, w, x, ln_g, ln_b)


def _rope_table_kernel(cos_ref, sin_ref, *, rows, seq, dec_seq):
    r = lax.broadcasted_iota(jnp.int32, (rows, ROPE_HALF), 0) + pl.program_id(0) * rows
    j = lax.broadcasted_iota(jnp.int32, (rows, ROPE_HALF), 1)
    pos = jnp.where(r < seq, r, PAST_LEN + ((r - seq) & (dec_seq - 1)))
    inv = jnp.power(np.float32(ROPE_BASE), -j.astype(F32) / ROPE_HALF)
    ang = pos.astype(F32) * inv
    cos_ref[...] = jnp.cos(ang)
    sin_ref[...] = jnp.sin(ang)


def _rope_tables(seq, dec_seq, rows):
    assert seq % rows == 0 and rows % dec_seq == 0 and dec_seq & (dec_seq - 1) == 0
    spec = pl.BlockSpec((rows, ROPE_HALF), lambda i: (i, 0))
    return pl.pallas_call(
        functools.partial(_rope_table_kernel, rows=rows, seq=seq, dec_seq=dec_seq),
        grid=(seq // rows + 1,),
        out_specs=[spec, spec],
        out_shape=[jax.ShapeDtypeStruct((seq + rows, ROPE_HALF), F32)] * 2,
        compiler_params=_params(("parallel",), 16),
        name="rope_tables",
    )()


def _ret_in_kernel(x_ref, w_ref, *rest, tn, rope, parts):
    if rope:
        cos_ref, sin_ref, o_ref, w_b = rest
    else:
        o_ref, w_b = rest

    @pl.when(pl.program_id(1) == 0)
    def _():
        w_b[...] = w_ref[...].astype(BF16)

    rows = x_ref.shape[0] // parts
    for p in range(parts):
        rs = slice(p * rows, (p + 1) * rows)
        acc = jnp.dot(x_ref[rs, :], w_b[...], preferred_element_type=F32)
        if not rope:
            o_ref[rs, :] = acc.astype(BF16)
            continue
        scale = jnp.where(pl.program_id(0) < RET_QK // tn, np.float32(RET_DK ** -0.5),
                          np.float32(1.0))
        cos = cos_ref[rs, :]
        sin = sin_ref[rs, :]
        for hh in range(tn // RET_DK):
            lo = slice(hh * RET_DK, hh * RET_DK + ROPE_HALF)
            hi = slice(hh * RET_DK + ROPE_HALF, (hh + 1) * RET_DK)
            a, b = acc[:, lo], acc[:, hi]
            o_ref[rs, lo] = ((a * cos - b * sin) * scale).astype(BF16)
            o_ref[rs, hi] = ((a * sin + b * cos) * scale).astype(BF16)


def _ret_in(xb, w, col0, n_cols, tm, parts, rope=None, tn=1024):
    m, d = xb.shape
    in_specs = [pl.BlockSpec((tm, d), lambda j, i: (i, 0)),
                pl.BlockSpec((None, d, tn), lambda j, i: (0, 0, col0 // tn + j))]
    args = [xb, w]
    if rope:
        cos, sin, n_prompt_rows, seq = rope
        assert cos.shape[0] == seq + tm and n_prompt_rows % tm == 0
        per_seq = seq // tm
        table_tile = lambda j, i: (jnp.where(i < n_prompt_rows // tm, i % per_seq, per_seq), 0)
        in_specs += [pl.BlockSpec((tm, ROPE_HALF), table_tile)] * 2
        args += [cos, sin]
    return pl.pallas_call(
        functools.partial(_ret_in_kernel, tn=tn, rope=bool(rope), parts=parts),
        grid=(n_cols // tn, m // tm),
        in_specs=in_specs,
        out_specs=pl.BlockSpec((tm, tn), lambda j, i: (i, j)),
        out_shape=jax.ShapeDtypeStruct((m, n_cols), BF16),
        scratch_shapes=[pltpu.VMEM((d, tn), BF16)],
        compiler_params=_params(("arbitrary", "arbitrary"), 60),
        name="ret_in_qk" if rope else "ret_in_vg",
    )(*args)


def _decay_tables(head, blk):
    def log_gamma(shape):
        hv = jnp.full(shape, head, dtype=jnp.int32).astype(F32)
        return jnp.log1p(-jnp.exp2(-5.0 - hv))

    row = lax.broadcasted_iota(jnp.int32, (blk, blk), 0).astype(F32)
    col = lax.broadcasted_iota(jnp.int32, (blk, blk), 1).astype(F32)
    diff = row - col
    decay = jnp.where(diff >= 0, jnp.exp(log_gamma((blk, blk)) * jnp.maximum(diff, 0.0)), 0.0)
    idx_v = lax.broadcasted_iota(jnp.int32, (blk, RET_DV), 0).astype(F32)
    idx_k = lax.broadcasted_iota(jnp.int32, (blk, RET_DK), 0).astype(F32)
    inner = jnp.exp(log_gamma((blk, RET_DV)) * (idx_v + 1.0))
    zeta = jnp.exp(log_gamma((blk, RET_DK)) * (blk - 1.0 - idx_k))
    blk_decay = jnp.exp(log_gamma((1, RET_DV)) * blk)
    return decay, inner, zeta, blk_decay


def _ret_block(qb, kb, vb, gb, state, tables):
    decay, inner, zeta, blk_decay = tables
    scores = lax.dot_general(qb, kb, (((1,), (1,)), ((), ())),
                             preferred_element_type=F32) * decay
    intra = jnp.dot(scores.astype(BF16), vb, preferred_element_type=F32)
    cross = jnp.dot(qb, state.astype(BF16), preferred_element_type=F32) * inner
    kz = (kb.astype(F32) * zeta).T.astype(BF16)
    new_state = blk_decay * state + jnp.dot(kz, vb, preferred_element_type=F32)
    o = intra + cross
    mu = jnp.mean(o, axis=-1, keepdims=True)
    oc = o - mu
    var = jnp.mean(oc * oc, axis=-1, keepdims=True)
    on = oc * lax.rsqrt(var + GN_EPS)
    y = (jax.nn.silu(gb.astype(F32)) * on).astype(BF16)
    return y, new_state


def _ret_kernel(q_ref, k_ref, v_ref, g_ref, qs_ref, ks_ref, vs_ref, gs_ref, s0_ref,
                y_ref, ys_ref, sp_ref, ss_ref, *, blk, n_blk, dec_seq, streams):
    tables = _decay_tables(pl.program_id(0) % RET_HEADS, blk)
    state = jnp.zeros((RET_DK, RET_DV), F32)
    for t in range(n_blk):
        rows = slice(t * blk, (t + 1) * blk)
        y, state = _ret_block(q_ref[rows, :], k_ref[rows, :], v_ref[rows, :],
                              g_ref[rows, :], state, tables)
        y_ref[rows, :] = y
    sp_ref[0, 0] = state
    for i in range(streams):
        rows = slice(i * dec_seq, (i + 1) * dec_seq)
        for h in range(RET_HEADS):
            ck = slice(h * RET_DK, (h + 1) * RET_DK)
            cv = slice(h * RET_DV, (h + 1) * RET_DV)
            y, new_state = _ret_block(qs_ref[rows, ck], ks_ref[rows, ck], vs_ref[rows, cv],
                                      gs_ref[rows, cv], s0_ref[0, i, h], _decay_tables(h, dec_seq))
            ys_ref[rows, cv] = y
            ss_ref[i, h] = new_state


def _retention(qk, vg, state0, batch, seq, dec_batch, dec_seq):
    blk = min(seq, RET_CHUNK)
    steps = batch * RET_HEADS
    assert dec_seq <= RET_BLOCK and dec_batch % steps == 0 and (batch * seq) % dec_seq == 0
    streams = dec_batch // steps
    srows = streams * dec_seq
    r_off = batch * seq // srows
    bh = lambda i: (i // RET_HEADS, i % RET_HEADS)
    state_shape = (RET_HEADS, RET_DK, RET_DV)
    return pl.pallas_call(
        functools.partial(_ret_kernel, blk=blk, n_blk=seq // blk, dec_seq=dec_seq,
                          streams=streams),
        grid=(steps,),
        in_specs=[pl.BlockSpec((seq, RET_DK), lambda i: bh(i)),
                  pl.BlockSpec((seq, RET_DK), lambda i: (i // RET_HEADS, RET_HEADS + i % RET_HEADS)),
                  pl.BlockSpec((seq, RET_DV), lambda i: bh(i)),
                  pl.BlockSpec((seq, RET_DV), lambda i: (i // RET_HEADS, RET_HEADS + i % RET_HEADS)),
                  pl.BlockSpec((srows, RET_QK), lambda i: (r_off + i, 0)),
                  pl.BlockSpec((srows, RET_QK), lambda i: (r_off + i, 1)),
                  pl.BlockSpec((srows, RET_V), lambda i: (r_off + i, 0)),
                  pl.BlockSpec((srows, RET_V), lambda i: (r_off + i, 1)),
                  pl.BlockSpec((1, streams) + state_shape, lambda i: (0, i, 0, 0, 0))],
        out_specs=[pl.BlockSpec((seq, RET_DV), lambda i: bh(i)),
                   pl.BlockSpec((srows, RET_V), lambda i: (i, 0)),
                   pl.BlockSpec((1, 1, RET_DK, RET_DV), lambda i: bh(i) + (0, 0)),
                   pl.BlockSpec((streams,) + state_shape, lambda i: (i, 0, 0, 0))],
        out_shape=[jax.ShapeDtypeStruct((batch * seq, RET_V), BF16),
                   jax.ShapeDtypeStruct((dec_batch * dec_seq, RET_V), BF16),
                   jax.ShapeDtypeStruct((batch, RET_HEADS, RET_DK, RET_DV), F32),
                   jax.ShapeDtypeStruct((dec_batch,) + state_shape, F32)],
        compiler_params=_params(("arbitrary",), 56),
        name="retention",
    )(qk, qk, vg, vg, qk, qk, vg, vg, state0)


def kernel(x_prompt, x_sample, state_ret, w_a_in, a_ln_g, a_ln_b, a_ws, a_bs, w_a_out,
           w_b_in, w_b_out, w_ffn_in, w_ffn_out, ln_mix_g, ln_mix_b, ln_ffn_g, ln_ffn_b):
    batch, seq, d = x_prompt.shape
    dec_batch, dec_seq, _ = x_sample.shape
    n_p = batch * seq
    n_s = dec_batch * dec_seq
    m = n_p + n_s
    assert d == D_MODEL and seq % GMLP_CHUNK == 0 and GMLP_CHUNK % dec_seq == 0
    assert w_a_in.shape[0] == 1 and w_b_in.shape[0] == 1 and state_ret.shape[0] == 1
    rows3 = lambda a: a.reshape(a.shape[0], 1, a.shape[1])
    a_ln_g, a_ln_b, ln_mix_g, ln_mix_b, ln_ffn_g, ln_ffn_b = map(
        rows3, (a_ln_g, a_ln_b, ln_mix_g, ln_mix_b, ln_ffn_g, ln_ffn_b))

    reps = GMLP_CHUNK // dec_seq
    expand = lambda b: jnp.repeat(b.T, GMLP_GROUP_DIM, axis=1)
    gmlp = functools.partial(_gmlp_layer, w_in=w_a_in, w_out=w_a_out, ln_g=a_ln_g, ln_b=a_ln_b,
                             mix_g=ln_mix_g, mix_b=ln_mix_b, m_total=m, dec_seq=dec_seq)
    x1, x1b = gmlp(x_prompt.reshape(n_p, d), wmix=a_ws[0], bias=expand(a_bs[0]), row_off=0)
    x1, x1b, v_sample = gmlp(
        x_sample.reshape(n_s, d), wmix=jnp.tile(a_ws[0][:, :dec_seq, :dec_seq], (1, reps, reps)),
        bias=expand(jnp.tile(a_bs[0][:, :dec_seq], (1, reps))), row_off=n_p, carry=(x1, x1b))
    act = _ffn_in(x1b, w_ffn_in, 0)
    x2, x2b = _proj_ln(act, w_ffn_out, 0, x1, ln_ffn_g, ln_ffn_b, 0, chunk_rows=512)

    cos, sin = _rope_tables(seq, dec_seq, rows=1024)
    qk = _ret_in(x2b, w_b_in, 0, 2 * RET_QK, tm=1024, parts=1, rope=(cos, sin, n_p, seq))
    vg = _ret_in(x2b, w_b_in, 2 * RET_QK, 2 * RET_V, tm=2304, parts=2)
    y_p, y_s, state_p, state_s = _retention(qk, vg, state_ret, batch, seq, dec_batch, dec_seq)
    x3, x3b = _proj_ln((y_p, y_s), w_b_out, 0, x2, ln_mix_g, ln_mix_b, 1, chunk_rows=512)
    act = _ffn_in(x3b, w_ffn_in, 1)
    y_prompt, y_sample = _proj_ln(act, w_ffn_out, 1, x3, ln_ffn_g, ln_ffn_b, 1, chunk_rows=512,
                                  split_rows=n_p)

    return (y_prompt.reshape(batch, seq, d), y_sample.reshape(dec_batch, dec_seq, d),
            state_p[None], state_s[None], v_sample.reshape(1, dec_batch, dec_seq, d))
```

```python
import functools
import math

import jax
import jax.numpy as jnp
import numpy as np
from jax import lax
from jax.experimental import pallas as pl
from jax.experimental.pallas import tpu as pltpu

D_MODEL = 2048
DEPTH = 2
PAST_LEN = 4096
RET_BLOCK = 64
RET_CHUNK = 256
GMLP_CHUNK = 128
GMLP_GROUPS = 8
GMLP_GROUP_DIM = D_MODEL // GMLP_GROUPS
RET_HEADS = 8
RET_DK = D_MODEL // RET_HEADS
RET_DV = 2 * RET_DK
RET_QK = RET_HEADS * RET_DK
RET_V = RET_HEADS * RET_DV
ROPE_BASE = 10000.0
ROPE_HALF = RET_DK // 2
ALPHA = (2 * DEPTH) ** 0.25
LN_EPS = 1e-5
GN_EPS = 1e-6

F32 = jnp.float32
BF16 = jnp.bfloat16
MIB = 1024 * 1024
ANY_SPEC = pl.BlockSpec(memory_space=pl.ANY)


def _params(semantics, vmem_mib):
    return pltpu.CompilerParams(dimension_semantics=semantics,
                                vmem_limit_bytes=vmem_mib * MIB)


def _layer_norm(x, g, b):
    mu = jnp.mean(x, axis=-1, keepdims=True)
    xc = x - mu
    var = jnp.mean(xc * xc, axis=-1, keepdims=True)
    return xc * lax.rsqrt(var + LN_EPS) * g + b


def _gelu(x):
    return 0.5 * x * (1.0 + lax.erf(x * np.float32(math.sqrt(0.5))))


def _stage_weight(step, w_ref, wb_ref, n_chunks):
    rows = w_ref.shape[0]

    @pl.when(step < n_chunks)
    def _():
        r0 = pl.multiple_of(step * rows, rows)
        wb_ref[pl.ds(r0, rows), :] = w_ref[...].astype(BF16)


def _row_spec(d, layer):
    return pl.BlockSpec((None, 1, d), lambda s: (layer, 0, 0))


def _chunk_spec(layer, rows, cols, n_chunks):
    return pl.BlockSpec((None, rows, cols),
                        lambda s: (layer, jnp.minimum(s, n_chunks - 1), 0))


def _gmlp_kernel(*refs, tm, n_stage, n_tiles, n_fill, is_sample, dec_seq):
    if is_sample:
        (x_ref, win_ref, wout_ref, lng_ref, lnb_ref, wmix_ref, bias_ref, g_ref, b_ref, _, _,
         x1_ref, x1b_ref, vn_ref, *scratch) = refs
    else:
        (x_ref, win_ref, wout_ref, lng_ref, lnb_ref, wmix_ref, bias_ref, g_ref, b_ref,
         x1_ref, x1b_ref, *scratch) = refs
    win_b, wout_b, u_ref, v_ref, s0, s1, xr0, xr1 = scratch
    step = pl.program_id(0)
    _stage_weight(step, win_ref, win_b, n_stage)
    _stage_weight(step, wout_ref, wout_b, n_stage)
    t = step - n_stage
    d = D_MODEL

    @pl.when(step < n_fill)
    def _():
        x1_ref[...] = jnp.zeros_like(x1_ref)
        x1b_ref[...] = jnp.zeros_like(x1b_ref)

    def gate_in(xr_ref):
        xf = x_ref[...]
        xr_ref[...] = xf
        xb = xf.astype(BF16)
        u_ref[...] = _gelu(jnp.dot(xb, win_b[:, :d], preferred_element_type=F32))
        v = _layer_norm(_gelu(jnp.dot(xb, win_b[:, d:], preferred_element_type=F32)),
                        lng_ref[...], lnb_ref[...])
        if is_sample:
            vn_ref[...] = v
        v_ref[...] = v.astype(BF16)

    def gate_mix(s_ref):
        row = lax.broadcasted_iota(jnp.int32, (GMLP_CHUNK, GMLP_CHUNK), 0)
        col = lax.broadcasted_iota(jnp.int32, (GMLP_CHUNK, GMLP_CHUNK), 1)
        mask = col <= row
        if is_sample:
            mask = mask & (((row ^ col) & ~(dec_seq - 1)) == 0)
        for g in range(GMLP_GROUPS):
            w_g = jnp.where(mask, wmix_ref[g], 0.0).astype(BF16)
            cs = slice(g * GMLP_GROUP_DIM, (g + 1) * GMLP_GROUP_DIM)
            for seg in range(tm // GMLP_CHUNK):
                rs = slice(seg * GMLP_CHUNK, (seg + 1) * GMLP_CHUNK)
                mixed = jnp.dot(w_g, v_ref[rs, cs], preferred_element_type=F32)
                mixed = mixed + bias_ref[:, cs]
                s_ref[rs, cs] = (u_ref[rs, cs] * mixed).astype(BF16)

    def project(s_ref):
        return jnp.dot(s_ref[...], wout_b[...], preferred_element_type=F32)

    def finish(h, xr_ref):
        x1 = _layer_norm(ALPHA * xr_ref[...] + h, g_ref[...], b_ref[...])
        x1_ref[...] = x1
        x1b_ref[...] = x1.astype(BF16)

    @pl.when(t == 0)
    def _():
        gate_in(xr0)
        gate_mix(s0)

    for parity, (cur, prev) in enumerate((((s0, xr0), (s1, xr1)), ((s1, xr1), (s0, xr0)))):
        @pl.when(((t & 1) == parity) & (t >= 1) & (t < n_tiles))
        def _(cur=cur, prev=prev):
            gate_in(cur[1])
            h = project(prev[0])
            gate_mix(cur[0])
            finish(h, prev[1])

    @pl.when(t == n_tiles)
    def _():
        last = (s0, xr0) if (n_tiles - 1) % 2 == 0 else (s1, xr1)
        finish(project(last[0]), last[1])


def _gmlp_layer(x, w_in, w_out, ln_g, ln_b, wmix, bias, mix_g, mix_b, m_total, row_off,
                dec_seq, carry=None, tm=256, n_stage=16):
    rows, d = x.shape
    is_sample = carry is not None
    n_tiles = rows // tm
    off = row_off // tm
    n_fill = 0 if is_sample else (m_total - rows) // tm
    assert n_fill <= n_stage and (is_sample or row_off == 0)
    gate_tile = lambda s: (jnp.clip(s - n_stage, 0, n_tiles - 1), 0)
    out_tile = lambda s: (jnp.where(s < n_fill, n_tiles + s,
                                    off + jnp.clip(s - n_stage - 1, 0, n_tiles - 1)), 0)
    row0 = _row_spec(d, 0)
    in_specs = [pl.BlockSpec((tm, d), gate_tile),
                _chunk_spec(0, d // n_stage, 2 * d, n_stage),
                _chunk_spec(0, d // n_stage, d, n_stage),
                row0, row0,
                pl.BlockSpec((GMLP_GROUPS, GMLP_CHUNK, GMLP_CHUNK), lambda s: (0, 0, 0)),
                pl.BlockSpec((GMLP_CHUNK, d), lambda s: (0, 0)),
                row0, row0]
    out_specs = [pl.BlockSpec((tm, d), out_tile), pl.BlockSpec((tm, d), out_tile)]
    out_shape = [jax.ShapeDtypeStruct((m_total, d), F32),
                 jax.ShapeDtypeStruct((m_total, d), BF16)]
    args = [x, w_in, w_out, ln_g, ln_b, wmix, bias, mix_g, mix_b]
    aliases = {}
    if is_sample:
        in_specs += [ANY_SPEC, ANY_SPEC]
        args += list(carry)
        aliases = {len(args) - 2: 0, len(args) - 1: 1}
        out_specs.append(pl.BlockSpec((tm, d), gate_tile))
        out_shape.append(jax.ShapeDtypeStruct((rows, d), F32))
    return pl.pallas_call(
        functools.partial(_gmlp_kernel, tm=tm, n_stage=n_stage, n_tiles=n_tiles, n_fill=n_fill,
                          is_sample=is_sample, dec_seq=dec_seq),
        grid=(n_stage + n_tiles + 1,),
        in_specs=in_specs,
        out_specs=out_specs,
        out_shape=out_shape,
        scratch_shapes=[pltpu.VMEM((d, 2 * d), BF16), pltpu.VMEM((d, d), BF16),
                        pltpu.VMEM((tm, d), F32), pltpu.VMEM((tm, d), BF16),
                        pltpu.VMEM((tm, d), BF16), pltpu.VMEM((tm, d), BF16),
                        pltpu.VMEM((tm, d), F32), pltpu.VMEM((tm, d), F32)],
        input_output_aliases=aliases,
        compiler_params=_params(("arbitrary",), 60),
        name="gmlp_sample" if is_sample else "gmlp_prompt",
    )(*args)


def _ffn_in_kernel(x_ref, wg_ref, wu_ref, o_ref, wg_b, wu_b, *, parts):
    @pl.when(pl.program_id(1) == 0)
    def _():
        wg_b[...] = wg_ref[...].astype(BF16)
        wu_b[...] = wu_ref[...].astype(BF16)

    rows = x_ref.shape[0] // parts
    for p in range(parts):
        rs = slice(p * rows, (p + 1) * rows)
        xb = x_ref[rs, :]
        gate = jnp.dot(xb, wg_b[...], preferred_element_type=F32)
        up = jnp.dot(xb, wu_b[...], preferred_element_type=F32)
        o_ref[rs, :] = (jax.nn.silu(gate) * up).astype(BF16)


def _ffn_in(xb, w_in, layer, tm=2304, th=512, parts=2):
    m, d = xb.shape
    hidden = w_in.shape[2] // 2
    nh = hidden // th
    return pl.pallas_call(
        functools.partial(_ffn_in_kernel, parts=parts),
        grid=(nh, m // tm),
        in_specs=[pl.BlockSpec((tm, d), lambda j, i: (i, 0)),
                  pl.BlockSpec((None, d, th), lambda j, i: (layer, 0, j)),
                  pl.BlockSpec((None, d, th), lambda j, i: (layer, 0, j + nh))],
        out_specs=pl.BlockSpec((tm, th), lambda j, i: (i, j)),
        out_shape=jax.ShapeDtypeStruct((m, hidden), BF16),
        scratch_shapes=[pltpu.VMEM((d, th), BF16), pltpu.VMEM((d, th), BF16)],
        compiler_params=_params(("arbitrary", "arbitrary"), 60),
        name="ffn_in",
    )(xb, w_in, w_in)


def _proj_ln_kernel(*refs, n_stage, n_tiles, src_split, out_split):
    n_a = 1 if src_split is None else 2
    a_refs, (w_ref, x_ref, g_ref, b_ref), rest = refs[:n_a], refs[n_a:n_a + 4], refs[n_a + 4:]
    outs, (w_b, h0, h1) = rest[:-3], rest[-3:]
    step = pl.program_id(0)
    _stage_weight(step, w_ref, w_b, n_stage)
    t = step - n_stage

    def matmul(dst, second_src):
        a_ref = a_refs[1 if second_src else 0]
        dst[...] = jnp.dot(a_ref[...], w_b[...], preferred_element_type=F32)

    def epilogue(src, second_out):
        x = _layer_norm(ALPHA * x_ref[...] + src[...], g_ref[...], b_ref[...])
        if out_split is None:
            outs[0][...] = x
            outs[1][...] = x.astype(BF16)
        else:
            outs[1 if second_out else 0][...] = x

    @pl.when(t == 0)
    def _():
        matmul(h0, False)

    for parity, (dst, src) in enumerate(((h0, h1), (h1, h0))):
        for second_src in ((False, True) if src_split is not None else (False,)):
            for second_out in ((False, True) if out_split is not None else (False,)):
                cond = ((t & 1) == parity) & (t >= 1) & (t < n_tiles)
                if src_split is not None:
                    cond &= (t >= src_split) if second_src else (t < src_split)
                if out_split is not None:
                    cond &= (t - 1 >= out_split) if second_out else (t - 1 < out_split)

                @pl.when(cond)
                def _(dst=dst, src=src, second_src=second_src, second_out=second_out):
                    matmul(dst, second_src)
                    epilogue(src, second_out)

    @pl.when(t == n_tiles)
    def _():
        epilogue(h0 if (n_tiles - 1) % 2 == 0 else h1, out_split is not None)


def _proj_ln(a, w, layer, x, ln_g, ln_b, ln_row, chunk_rows, split_rows=None, tm=256):
    a_parts = a if isinstance(a, tuple) else (a,)
    kdim = a_parts[0].shape[1]
    m = sum(p.shape[0] for p in a_parts)
    d = w.shape[2]
    n_stage = kdim // chunk_rows
    n_tiles = m // tm
    clip = lambda v, n: jnp.clip(v, 0, n - 1)
    ln_tile = lambda s: (clip(s - n_stage - 1, n_tiles), 0)
    if len(a_parts) == 1:
        src_split = None
        a_specs = [pl.BlockSpec((tm, kdim), lambda s: (clip(s - n_stage, n_tiles), 0))]
    else:
        src_split = a_parts[0].shape[0] // tm
        assert 1 <= src_split < n_tiles and a_parts[0].shape[0] % tm == 0
        a_specs = [pl.BlockSpec((tm, kdim), lambda s: (clip(s - n_stage, src_split), 0)),
                   pl.BlockSpec((tm, kdim),
                                lambda s: (clip(s - n_stage - src_split, n_tiles - src_split), 0))]
    in_specs = a_specs + [_chunk_spec(layer, chunk_rows, d, n_stage),
                          pl.BlockSpec((tm, d), ln_tile),
                          _row_spec(d, ln_row), _row_spec(d, ln_row)]
    if split_rows is None:
        out_split = None
        out_specs = [pl.BlockSpec((tm, d), ln_tile), pl.BlockSpec((tm, d), ln_tile)]
        out_shape = [jax.ShapeDtypeStruct((m, d), F32), jax.ShapeDtypeStruct((m, d), BF16)]
    else:
        out_split = split_rows // tm
        assert 1 <= out_split < n_tiles and split_rows % tm == 0
        out_specs = [
            pl.BlockSpec((tm, d), lambda s: (clip(s - n_stage - 1, out_split), 0)),
            pl.BlockSpec((tm, d), lambda s: (clip(s - n_stage - 1 - out_split, n_tiles - out_split), 0))]
        out_shape = [jax.ShapeDtypeStruct((split_rows, d), F32),
                     jax.ShapeDtypeStruct((m - split_rows, d), F32)]
    return pl.pallas_call(
        functools.partial(_proj_ln_kernel, n_stage=n_stage, n_tiles=n_tiles,
                          src_split=src_split, out_split=out_split),
        grid=(n_stage + n_tiles + 1,),
        in_specs=in_specs,
        out_specs=out_specs,
        out_shape=out_shape,
        scratch_shapes=[pltpu.VMEM((kdim, d), BF16), pltpu.VMEM((tm, d), F32),
                        pltpu.VMEM((tm, d), F32)],
        compiler_params=_params(("arbitrary",), 60),
        name="proj_ln_split" if split_rows else "proj_ln",
    )(*a_parts, w, x, ln_g, ln_b)


def _rope_table_kernel(cos_ref, sin_ref, *, rows, seq, dec_seq):
    r = lax.broadcasted_iota(jnp.int32, (rows, ROPE_HALF), 0) + pl.program_id(0) * rows
    j = lax.broadcasted_iota(jnp.int32, (rows, ROPE_HALF), 1)
    pos = jnp.where(r < seq, r, PAST_LEN + ((r - seq) & (dec_seq - 1)))
    inv = jnp.power(np.float32(ROPE_BASE), -j.astype(F32) / ROPE_HALF)
    ang = pos.astype(F32) * inv
    cos_ref[...] = jnp.cos(ang)
    sin_ref[...] = jnp.sin(ang)


def _rope_tables(seq, dec_seq, rows):
    assert seq % rows == 0 and rows % dec_seq == 0 and dec_seq & (dec_seq - 1) == 0
    spec = pl.BlockSpec((rows, ROPE_HALF), lambda i: (i, 0))
    return pl.pallas_call(
        functools.partial(_rope_table_kernel, rows=rows, seq=seq, dec_seq=dec_seq),
        grid=(seq // rows + 1,),
        out_specs=[spec, spec],
        out_shape=[jax.ShapeDtypeStruct((seq + rows, ROPE_HALF), F32)] * 2,
        compiler_params=_params(("parallel",), 16),
        name="rope_tables",
    )()


def _ret_in_kernel(x_ref, w_ref, *rest, tn, rope, parts):
    if rope:
        cos_ref, sin_ref, o_ref, w_b = rest
    else:
        o_ref, w_b = rest

    @pl.when(pl.program_id(1) == 0)
    def _():
        w_b[...] = w_ref[...].astype(BF16)

    rows = x_ref.shape[0] // parts
    heads, _, width = o_ref.shape
    for p in range(parts):
        rs = slice(p * rows, (p + 1) * rows)
        acc = jnp.dot(x_ref[rs, :], w_b[...], preferred_element_type=F32)
        if not rope:
            for hh in range(heads):
                o_ref[hh, rs, :] = acc[:, hh * width:(hh + 1) * width].astype(BF16)
            continue
        scale = jnp.where(pl.program_id(0) < RET_QK // tn, np.float32(RET_DK ** -0.5),
                          np.float32(1.0))
        cos = cos_ref[rs, :]
        sin = sin_ref[rs, :]
        for hh in range(heads):
            a = acc[:, hh * RET_DK:hh * RET_DK + ROPE_HALF]
            b = acc[:, hh * RET_DK + ROPE_HALF:(hh + 1) * RET_DK]
            o_ref[hh, rs, :ROPE_HALF] = ((a * cos - b * sin) * scale).astype(BF16)
            o_ref[hh, rs, ROPE_HALF:] = ((a * sin + b * cos) * scale).astype(BF16)


def _ret_in(xb, w, col0, n_cols, width, tm, parts, rope=None, tn=1024):
    m, d = xb.shape
    assert tn % width == 0 and (not rope or width == RET_DK)
    in_specs = [pl.BlockSpec((tm, d), lambda j, i: (i, 0)),
                pl.BlockSpec((None, d, tn), lambda j, i: (0, 0, col0 // tn + j))]
    args = [xb, w]
    if rope:
        cos, sin, n_prompt_rows, seq = rope
        assert cos.shape[0] == seq + tm and n_prompt_rows % tm == 0
        per_seq = seq // tm
        table_tile = lambda j, i: (jnp.where(i < n_prompt_rows // tm, i % per_seq, per_seq), 0)
        in_specs += [pl.BlockSpec((tm, ROPE_HALF), table_tile)] * 2
        args += [cos, sin]
    return pl.pallas_call(
        functools.partial(_ret_in_kernel, tn=tn, rope=bool(rope), parts=parts),
        grid=(n_cols // tn, m // tm),
        in_specs=in_specs,
        out_specs=pl.BlockSpec((tn // width, tm, width), lambda j, i: (j, i, 0)),
        out_shape=jax.ShapeDtypeStruct((n_cols // width, m, width), BF16),
        scratch_shapes=[pltpu.VMEM((d, tn), BF16)],
        compiler_params=_params(("arbitrary", "arbitrary"), 60),
        name="ret_in_qk" if rope else "ret_in_vg",
    )(*args)


def _decay_tables(head, blk):
    def log_gamma(shape):
        hv = jnp.full(shape, head, dtype=jnp.int32).astype(F32)
        return jnp.log1p(-jnp.exp2(-5.0 - hv))

    row = lax.broadcasted_iota(jnp.int32, (blk, blk), 0).astype(F32)
    col = lax.broadcasted_iota(jnp.int32, (blk, blk), 1).astype(F32)
    diff = row - col
    decay = jnp.where(diff >= 0, jnp.exp(log_gamma((blk, blk)) * jnp.maximum(diff, 0.0)), 0.0)
    idx_v = lax.broadcasted_iota(jnp.int32, (blk, RET_DV), 0).astype(F32)
    idx_k = lax.broadcasted_iota(jnp.int32, (blk, RET_DK), 0).astype(F32)
    inner = jnp.exp(log_gamma((blk, RET_DV)) * (idx_v + 1.0))
    zeta = jnp.exp(log_gamma((blk, RET_DK)) * (blk - 1.0 - idx_k))
    blk_decay = jnp.exp(log_gamma((1, RET_DV)) * blk)
    return decay, inner, zeta, blk_decay


def _ret_block(qb, kb, vb, gb, state, tables):
    decay, inner, zeta, blk_decay = tables
    scores = lax.dot_general(qb, kb, (((1,), (1,)), ((), ())),
                             preferred_element_type=F32) * decay
    intra = jnp.dot(scores.astype(BF16), vb, preferred_element_type=F32)
    cross = jnp.dot(qb, state.astype(BF16), preferred_element_type=F32) * inner
    kz = (kb.astype(F32) * zeta).T.astype(BF16)
    new_state = blk_decay * state + jnp.dot(kz, vb, preferred_element_type=F32)
    o = intra + cross
    mu = jnp.mean(o, axis=-1, keepdims=True)
    oc = o - mu
    var = jnp.mean(oc * oc, axis=-1, keepdims=True)
    on = oc * lax.rsqrt(var + GN_EPS)
    y = (jax.nn.silu(gb.astype(F32)) * on).astype(BF16)
    return y, new_state


def _ret_kernel(q_ref, k_ref, v_ref, g_ref, qs_ref, ks_ref, vs_ref, gs_ref, s0_ref,
                y_ref, ys_ref, sp_ref, ss_ref, *, blk, n_blk, dec_seq, streams):
    tables = _decay_tables(pl.program_id(0) % RET_HEADS, blk)
    state = jnp.zeros((RET_DK, RET_DV), F32)
    for t in range(n_blk):
        rows = slice(t * blk, (t + 1) * blk)
        y, state = _ret_block(q_ref[rows, :], k_ref[rows, :], v_ref[rows, :],
                              g_ref[rows, :], state, tables)
        y_ref[rows, :] = y
    sp_ref[0, 0] = state
    for i in range(streams):
        rows = slice(i * dec_seq, (i + 1) * dec_seq)
        for h in range(RET_HEADS):
            cv = slice(h * RET_DV, (h + 1) * RET_DV)
            y, new_state = _ret_block(qs_ref[h, rows, :], ks_ref[h, rows, :], vs_ref[h, rows, :],
                                      gs_ref[h, rows, :], s0_ref[0, i, h], _decay_tables(h, dec_seq))
            ys_ref[rows, cv] = y
            ss_ref[i, h] = new_state


def _retention(qk, vg, state0, batch, seq, dec_batch, dec_seq):
    blk = min(seq, RET_CHUNK)
    steps = batch * RET_HEADS
    assert dec_seq <= RET_BLOCK and dec_batch % steps == 0 and (batch * seq) % dec_seq == 0
    streams = dec_batch // steps
    srows = streams * dec_seq
    r_off = batch * seq // srows
    bh = lambda i: (i // RET_HEADS, i % RET_HEADS)
    state_shape = (RET_HEADS, RET_DK, RET_DV)
    return pl.pallas_call(
        functools.partial(_ret_kernel, blk=blk, n_blk=seq // blk, dec_seq=dec_seq,
                          streams=streams),
        grid=(steps,),
        in_specs=[pl.BlockSpec((None, seq, RET_DK), lambda i: (i % RET_HEADS, i // RET_HEADS, 0)),
                  pl.BlockSpec((None, seq, RET_DK),
                               lambda i: (RET_HEADS + i % RET_HEADS, i // RET_HEADS, 0)),
                  pl.BlockSpec((None, seq, RET_DV), lambda i: (i % RET_HEADS, i // RET_HEADS, 0)),
                  pl.BlockSpec((None, seq, RET_DV),
                               lambda i: (RET_HEADS + i % RET_HEADS, i // RET_HEADS, 0)),
                  pl.BlockSpec((RET_HEADS, srows, RET_DK), lambda i: (0, r_off + i, 0)),
                  pl.BlockSpec((RET_HEADS, srows, RET_DK), lambda i: (1, r_off + i, 0)),
                  pl.BlockSpec((RET_HEADS, srows, RET_DV), lambda i: (0, r_off + i, 0)),
                  pl.BlockSpec((RET_HEADS, srows, RET_DV), lambda i: (1, r_off + i, 0)),
                  pl.BlockSpec((1, streams) + state_shape, lambda i: (0, i, 0, 0, 0))],
        out_specs=[pl.BlockSpec((seq, RET_DV), lambda i: bh(i)),
                   pl.BlockSpec((srows, RET_V), lambda i: (i, 0)),
                   pl.BlockSpec((1, 1, RET_DK, RET_DV), lambda i: bh(i) + (0, 0)),
                   pl.BlockSpec((streams,) + state_shape, lambda i: (i, 0, 0, 0))],
        out_shape=[jax.ShapeDtypeStruct((batch * seq, RET_V), BF16),
                   jax.ShapeDtypeStruct((dec_batch * dec_seq, RET_V), BF16),
                   jax.ShapeDtypeStruct((batch, RET_HEADS, RET_DK, RET_DV), F32),
                   jax.ShapeDtypeStruct((dec_batch,) + state_shape, F32)],
        compiler_params=_params(("arbitrary",), 56),
        name="retention",
    )(qk, qk, vg, vg, qk, qk, vg, vg, state0)


def kernel(x_prompt, x_sample, state_ret, w_a_in, a_ln_g, a_ln_b, a_ws, a_bs, w_a_out,
           w_b_in, w_b_out, w_ffn_in, w_ffn_out, ln_mix_g, ln_mix_b, ln_ffn_g, ln_ffn_b):
    batch, seq, d = x_prompt.shape
    dec_batch, dec_seq, _ = x_sample.shape
    n_p = batch * seq
    n_s = dec_batch * dec_seq
    m = n_p + n_s
    assert d == D_MODEL and seq % GMLP_CHUNK == 0 and GMLP_CHUNK % dec_seq == 0
    assert w_a_in.shape[0] == 1 and w_b_in.shape[0] == 1 and state_ret.shape[0] == 1
    rows3 = lambda a: a.reshape(a.shape[0], 1, a.shape[1])
    a_ln_g, a_ln_b, ln_mix_g, ln_mix_b, ln_ffn_g, ln_ffn_b = map(
        rows3, (a_ln_g, a_ln_b, ln_mix_g, ln_mix_b, ln_ffn_g, ln_ffn_b))

    reps = GMLP_CHUNK // dec_seq
    expand = lambda b: jnp.repeat(b.T, GMLP_GROUP_DIM, axis=1)
    gmlp = functools.partial(_gmlp_layer, w_in=w_a_in, w_out=w_a_out, ln_g=a_ln_g, ln_b=a_ln_b,
                             mix_g=ln_mix_g, mix_b=ln_mix_b, m_total=m, dec_seq=dec_seq)
    x1, x1b = gmlp(x_prompt.reshape(n_p, d), wmix=a_ws[0], bias=expand(a_bs[0]), row_off=0)
    x1, x1b, v_sample = gmlp(
        x_sample.reshape(n_s, d), wmix=jnp.tile(a_ws[0][:, :dec_seq, :dec_seq], (1, reps, reps)),
        bias=expand(jnp.tile(a_bs[0][:, :dec_seq], (1, reps))), row_off=n_p, carry=(x1, x1b))
    act = _ffn_in(x1b, w_ffn_in, 0)
    x2, x2b = _proj_ln(act, w_ffn_out, 0, x1, ln_ffn_g, ln_ffn_b, 0, chunk_rows=512)

    cos, sin = _rope_tables(seq, dec_seq, rows=1024)
    qk = _ret_in(x2b, w_b_in, 0, 2 * RET_QK, RET_DK, tm=1024, parts=2, rope=(cos, sin, n_p, seq))
    vg = _ret_in(x2b, w_b_in, 2 * RET_QK, 2 * RET_V, RET_DV, tm=2304, parts=2)
    y_p, y_s, state_p, state_s = _retention(qk, vg, state_ret, batch, seq, dec_batch, dec_seq)
    x3, x3b = _proj_ln((y_p, y_s), w_b_out, 0, x2, ln_mix_g, ln_mix_b, 1, chunk_rows=512)
    act = _ffn_in(x3b, w_ffn_in, 1)
    y_prompt, y_sample = _proj_ln(act, w_ffn_out, 1, x3, ln_ffn_g, ln_ffn_b, 1, chunk_rows=512,
                                  split_rows=n_p)

    return (y_prompt.reshape(batch, seq, d), y_sample.reshape(dec_batch, dec_seq, d),
            state_p[None], state_s[None], v_sample.reshape(1, dec_batch, dec_seq, d))
```

```python
import functools
import math

import jax
import jax.numpy as jnp
import numpy as np
from jax import lax
from jax.experimental import pallas as pl
from jax.experimental.pallas import tpu as pltpu

D_MODEL = 2048
DEPTH = 2
PAST_LEN = 4096
RET_BLOCK = 64
RET_CHUNK = 256
GMLP_CHUNK = 128
GMLP_GROUPS = 8
GMLP_GROUP_DIM = D_MODEL // GMLP_GROUPS
RET_HEADS = 8
RET_DK = D_MODEL // RET_HEADS
RET_DV = 2 * RET_DK
RET_QK = RET_HEADS * RET_DK
RET_V = RET_HEADS * RET_DV
ROPE_BASE = 10000.0
ROPE_HALF = RET_DK // 2
ALPHA = (2 * DEPTH) ** 0.25
LN_EPS = 1e-5
GN_EPS = 1e-6

F32 = jnp.float32
BF16 = jnp.bfloat16
MIB = 1024 * 1024


def _params(semantics, vmem_mib):
    return pltpu.CompilerParams(dimension_semantics=semantics,
                                vmem_limit_bytes=vmem_mib * MIB)


def _layer_norm(x, g, b):
    mu = jnp.mean(x, axis=-1, keepdims=True)
    xc = x - mu
    var = jnp.mean(xc * xc, axis=-1, keepdims=True)
    return xc * lax.rsqrt(var + LN_EPS) * g + b


def _gelu(x):
    return 0.5 * x * (1.0 + lax.erf(x * np.float32(math.sqrt(0.5))))


def _stage_weight(step, w_ref, wb_ref, n_chunks):
    rows = w_ref.shape[0]

    @pl.when(step < n_chunks)
    def _():
        r0 = pl.multiple_of(step * rows, rows)
        wb_ref[pl.ds(r0, rows), :] = w_ref[...].astype(BF16)


def _whole(a):
    return pl.BlockSpec(a.shape, lambda s: (0,) * a.ndim)


def _chunk_spec(layer, rows, cols, n_chunks):
    return pl.BlockSpec((None, rows, cols),
                        lambda s: (layer, jnp.minimum(s, n_chunks - 1), 0))


def _gmlp_kernel(xp_ref, xs_ref, win_ref, wout_ref, lng_ref, lnb_ref, wmix_ref, bias_ref, g_ref,
                 b_ref, x1_ref, x1b_ref, vn_ref, win_b, wout_b, u_ref, v_ref, s0, s1, xr0, xr1,
                 *, tm, n_stage, n_tiles, n_prompt_tiles, dec_seq):
    step = pl.program_id(0)
    _stage_weight(step, win_ref, win_b, n_stage)
    _stage_weight(step, wout_ref, wout_b, n_stage)
    t = step - n_stage
    d = D_MODEL

    def gate_in(xr_ref, sample):
        xf = (xs_ref if sample else xp_ref)[...]
        xr_ref[...] = xf
        xb = xf.astype(BF16)
        u_ref[...] = _gelu(jnp.dot(xb, win_b[:, :d], preferred_element_type=F32))
        v = _layer_norm(_gelu(jnp.dot(xb, win_b[:, d:], preferred_element_type=F32)),
                        lng_ref[...], lnb_ref[...])
        if sample:
            vn_ref[...] = v
        v_ref[...] = v.astype(BF16)

    def gate_mix(s_ref, sample):
        row = lax.broadcasted_iota(jnp.int32, (GMLP_CHUNK, GMLP_CHUNK), 0)
        col = lax.broadcasted_iota(jnp.int32, (GMLP_CHUNK, GMLP_CHUNK), 1)
        mask = col <= row
        if sample:
            mask = mask & (((row ^ col) & ~(dec_seq - 1)) == 0)
        which = 1 if sample else 0
        for g in range(GMLP_GROUPS):
            w_g = jnp.where(mask, wmix_ref[which, g], 0.0).astype(BF16)
            cs = slice(g * GMLP_GROUP_DIM, (g + 1) * GMLP_GROUP_DIM)
            for seg in range(tm // GMLP_CHUNK):
                rs = slice(seg * GMLP_CHUNK, (seg + 1) * GMLP_CHUNK)
                mixed = jnp.dot(w_g, v_ref[rs, cs], preferred_element_type=F32)
                mixed = mixed + bias_ref[which, :, cs]
                s_ref[rs, cs] = (u_ref[rs, cs] * mixed).astype(BF16)

    def project(s_ref):
        return jnp.dot(s_ref[...], wout_b[...], preferred_element_type=F32)

    def finish(h, xr_ref):
        x1 = _layer_norm(ALPHA * xr_ref[...] + h, g_ref[0:1, :], b_ref[0:1, :])
        x1_ref[...] = x1
        x1b_ref[...] = x1.astype(BF16)

    @pl.when(t == 0)
    def _():
        gate_in(xr0, False)
        gate_mix(s0, False)

    for parity, (cur, prev) in enumerate((((s0, xr0), (s1, xr1)), ((s1, xr1), (s0, xr0)))):
        for sample in (False, True):
            cond = ((t & 1) == parity) & (t >= 1) & (t < n_tiles)
            cond &= (t >= n_prompt_tiles) if sample else (t < n_prompt_tiles)

            @pl.when(cond)
            def _(cur=cur, prev=prev, sample=sample):
                gate_in(cur[1], sample)
                h = project(prev[0])
                gate_mix(cur[0], sample)
                finish(h, prev[1])

    @pl.when(t == n_tiles)
    def _():
        last = (s0, xr0) if (n_tiles - 1) % 2 == 0 else (s1, xr1)
        finish(project(last[0]), last[1])


def _gmlp_layer(xp, xs, w_in, w_out, ln_g, ln_b, wmix, bias, mix_g, mix_b, dec_seq,
                tm=256, n_stage=32):
    (n_p, d), n_s = xp.shape, xs.shape[0]
    n_prompt_tiles, n_sample_tiles = n_p // tm, n_s // tm
    n_tiles = n_prompt_tiles + n_sample_tiles
    assert n_p % tm == 0 and n_s % tm == 0 and n_sample_tiles >= 1
    clip = lambda v, n: jnp.clip(v, 0, n - 1)
    sample_tile = lambda s: (clip(s - n_stage - n_prompt_tiles, n_sample_tiles), 0)
    out_tile = lambda s: (clip(s - n_stage - 1, n_tiles), 0)
    once = pl.Buffered(1)
    in_specs = [pl.BlockSpec((tm, d), lambda s: (clip(s - n_stage, n_prompt_tiles), 0)),
                pl.BlockSpec((tm, d), sample_tile),
                _chunk_spec(0, d // n_stage, 2 * d, n_stage),
                _chunk_spec(0, d // n_stage, d, n_stage),
                _whole(ln_g), _whole(ln_b),
                pl.BlockSpec((2, GMLP_GROUPS, GMLP_CHUNK, GMLP_CHUNK), lambda s: (0, 0, 0, 0),
                             pipeline_mode=once),
                pl.BlockSpec((2, GMLP_CHUNK, d), lambda s: (0, 0, 0), pipeline_mode=once),
                _whole(mix_g), _whole(mix_b)]
    return pl.pallas_call(
        functools.partial(_gmlp_kernel, tm=tm, n_stage=n_stage, n_tiles=n_tiles,
                          n_prompt_tiles=n_prompt_tiles, dec_seq=dec_seq),
        grid=(n_stage + n_tiles + 1,),
        in_specs=in_specs,
        out_specs=[pl.BlockSpec((tm, d), out_tile), pl.BlockSpec((tm, d), out_tile),
                   pl.BlockSpec((tm, d), sample_tile)],
        out_shape=[jax.ShapeDtypeStruct((n_p + n_s, d), F32),
                   jax.ShapeDtypeStruct((n_p + n_s, d), BF16),
                   jax.ShapeDtypeStruct((n_s, d), F32)],
        scratch_shapes=[pltpu.VMEM((d, 2 * d), BF16), pltpu.VMEM((d, d), BF16),
                        pltpu.VMEM((tm, d), F32), pltpu.VMEM((tm, d), BF16),
                        pltpu.VMEM((tm, d), BF16), pltpu.VMEM((tm, d), BF16),
                        pltpu.VMEM((tm, d), F32), pltpu.VMEM((tm, d), F32)],
        compiler_params=_params(("arbitrary",), 62),
        name="gmlp",
    )(xp, xs, w_in, w_out, ln_g, ln_b, wmix, bias, mix_g, mix_b)


def _ffn_in_kernel(x_ref, wg_ref, wu_ref, o_ref, wg_b, wu_b, *, parts):
    @pl.when(pl.program_id(1) == 0)
    def _():
        wg_b[...] = wg_ref[...].astype(BF16)
        wu_b[...] = wu_ref[...].astype(BF16)

    rows = x_ref.shape[0] // parts
    for p in range(parts):
        rs = slice(p * rows, (p + 1) * rows)
        xb = x_ref[rs, :]
        gate = jnp.dot(xb, wg_b[...], preferred_element_type=F32)
        up = jnp.dot(xb, wu_b[...], preferred_element_type=F32)
        o_ref[rs, :] = (jax.nn.silu(gate) * up).astype(BF16)


def _ffn_in(xb, w_in, layer, tm=2304, th=512, parts=2):
    m, d = xb.shape
    hidden = w_in.shape[2] // 2
    nh = hidden // th
    return pl.pallas_call(
        functools.partial(_ffn_in_kernel, parts=parts),
        grid=(nh, m // tm),
        in_specs=[pl.BlockSpec((tm, d), lambda j, i: (i, 0)),
                  pl.BlockSpec((None, d, th), lambda j, i: (layer, 0, j)),
                  pl.BlockSpec((None, d, th), lambda j, i: (layer, 0, j + nh))],
        out_specs=pl.BlockSpec((tm, th), lambda j, i: (i, j)),
        out_shape=jax.ShapeDtypeStruct((m, hidden), BF16),
        scratch_shapes=[pltpu.VMEM((d, th), BF16), pltpu.VMEM((d, th), BF16)],
        compiler_params=_params(("arbitrary", "arbitrary"), 60),
        name="ffn_in",
    )(xb, w_in, w_in)


def _proj_ln_kernel(*refs, n_stage, n_tiles, src_split, out_split, ln_row):
    n_a = 1 if src_split is None else 2
    a_refs, (w_ref, x_ref, g_ref, b_ref), rest = refs[:n_a], refs[n_a:n_a + 4], refs[n_a + 4:]
    outs, (w_b, h0, h1) = rest[:-3], rest[-3:]
    step = pl.program_id(0)
    _stage_weight(step, w_ref, w_b, n_stage)
    t = step - n_stage

    def matmul(dst, second_src):
        a_ref = a_refs[1 if second_src else 0]
        dst[...] = jnp.dot(a_ref[...], w_b[...], preferred_element_type=F32)

    def epilogue(src, second_out):
        ln = slice(ln_row, ln_row + 1)
        x = _layer_norm(ALPHA * x_ref[...] + src[...], g_ref[ln, :], b_ref[ln, :])
        if out_split is None:
            outs[0][...] = x
            outs[1][...] = x.astype(BF16)
        else:
            outs[1 if second_out else 0][...] = x

    @pl.when(t == 0)
    def _():
        matmul(h0, False)

    for parity, (dst, src) in enumerate(((h0, h1), (h1, h0))):
        for second_src in ((False, True) if src_split is not None else (False,)):
            for second_out in ((False, True) if out_split is not None else (False,)):
                cond = ((t & 1) == parity) & (t >= 1) & (t < n_tiles)
                if src_split is not None:
                    cond &= (t >= src_split) if second_src else (t < src_split)
                if out_split is not None:
                    cond &= (t - 1 >= out_split) if second_out else (t - 1 < out_split)

                @pl.when(cond)
                def _(dst=dst, src=src, second_src=second_src, second_out=second_out):
                    matmul(dst, second_src)
                    epilogue(src, second_out)

    @pl.when(t == n_tiles)
    def _():
        epilogue(h0 if (n_tiles - 1) % 2 == 0 else h1, out_split is not None)


def _proj_ln(a, w, layer, x, ln_g, ln_b, ln_row, chunk_rows, split_rows=None, tm=256):
    a_parts = a if isinstance(a, tuple) else (a,)
    kdim = a_parts[0].shape[1]
    m = sum(p.shape[0] for p in a_parts)
    d = w.shape[2]
    n_stage = kdim // chunk_rows
    n_tiles = m // tm
    clip = lambda v, n: jnp.clip(v, 0, n - 1)
    ln_tile = lambda s: (clip(s - n_stage - 1, n_tiles), 0)
    if len(a_parts) == 1:
        src_split = None
        a_specs = [pl.BlockSpec((tm, kdim), lambda s: (clip(s - n_stage, n_tiles), 0))]
    else:
        src_split = a_parts[0].shape[0] // tm
        assert 1 <= src_split < n_tiles and a_parts[0].shape[0] % tm == 0
        a_specs = [pl.BlockSpec((tm, kdim), lambda s: (clip(s - n_stage, src_split), 0)),
                   pl.BlockSpec((tm, kdim),
                                lambda s: (clip(s - n_stage - src_split, n_tiles - src_split), 0))]
    in_specs = a_specs + [_chunk_spec(layer, chunk_rows, d, n_stage),
                          pl.BlockSpec((tm, d), ln_tile),
                          _whole(ln_g), _whole(ln_b)]
    if split_rows is None:
        out_split = None
        out_specs = [pl.BlockSpec((tm, d), ln_tile), pl.BlockSpec((tm, d), ln_tile)]
        out_shape = [jax.ShapeDtypeStruct((m, d), F32), jax.ShapeDtypeStruct((m, d), BF16)]
    else:
        out_split = split_rows // tm
        assert 1 <= out_split < n_tiles and split_rows % tm == 0
        out_specs = [
            pl.BlockSpec((tm, d), lambda s: (clip(s - n_stage - 1, out_split), 0)),
            pl.BlockSpec((tm, d), lambda s: (clip(s - n_stage - 1 - out_split, n_tiles - out_split), 0))]
        out_shape = [jax.ShapeDtypeStruct((split_rows, d), F32),
                     jax.ShapeDtypeStruct((m - split_rows, d), F32)]
    return pl.pallas_call(
        functools.partial(_proj_ln_kernel, n_stage=n_stage, n_tiles=n_tiles,
                          src_split=src_split, out_split=out_split, ln_row=ln_row),
        grid=(n_stage + n_tiles + 1,),
        in_specs=in_specs,
        out_specs=out_specs,
        out_shape=out_shape,
        scratch_shapes=[pltpu.VMEM((kdim, d), BF16), pltpu.VMEM((tm, d), F32),
                        pltpu.VMEM((tm, d), F32)],
        compiler_params=_params(("arbitrary",), 60),
        name="proj_ln_split" if split_rows else "proj_ln",
    )(*a_parts, w, x, ln_g, ln_b)


def _rope_table_kernel(cos_ref, sin_ref, *, rows, seq, dec_seq):
    r = lax.broadcasted_iota(jnp.int32, (rows, ROPE_HALF), 0) + pl.program_id(0) * rows
    j = lax.broadcasted_iota(jnp.int32, (rows, ROPE_HALF), 1)
    pos = jnp.where(r < seq, r, PAST_LEN + ((r - seq) & (dec_seq - 1)))
    inv = jnp.power(np.float32(ROPE_BASE), -j.astype(F32) / ROPE_HALF)
    ang = pos.astype(F32) * inv
    cos_ref[...] = jnp.cos(ang)
    sin_ref[...] = jnp.sin(ang)


def _rope_tables(seq, dec_seq, rows):
    assert seq % rows == 0 and rows % dec_seq == 0 and dec_seq & (dec_seq - 1) == 0
    spec = pl.BlockSpec((rows, ROPE_HALF), lambda i: (i, 0))
    return pl.pallas_call(
        functools.partial(_rope_table_kernel, rows=rows, seq=seq, dec_seq=dec_seq),
        grid=(seq // rows + 1,),
        out_specs=[spec, spec],
        out_shape=[jax.ShapeDtypeStruct((seq + rows, ROPE_HALF), F32)] * 2,
        compiler_params=_params(("parallel",), 16),
        name="rope_tables",
    )()


def _ret_in_kernel(x_ref, w_ref, *rest, tn, rope, parts):
    if rope:
        cos_ref, sin_ref, o_ref, w_b = rest
    else:
        o_ref, w_b = rest

    @pl.when(pl.program_id(1) == 0)
    def _():
        w_b[...] = w_ref[...].astype(BF16)

    rows = x_ref.shape[0] // parts
    heads, _, width = o_ref.shape
    for p in range(parts):
        rs = slice(p * rows, (p + 1) * rows)
        acc = jnp.dot(x_ref[rs, :], w_b[...], preferred_element_type=F32)
        if not rope:
            for hh in range(heads):
                o_ref[hh, rs, :] = acc[:, hh * width:(hh + 1) * width].astype(BF16)
            continue
        scale = jnp.where(pl.program_id(0) < RET_QK // tn, np.float32(RET_DK ** -0.5),
                          np.float32(1.0))
        cos = cos_ref[rs, :]
        sin = sin_ref[rs, :]
        for hh in range(heads):
            a = acc[:, hh * RET_DK:hh * RET_DK + ROPE_HALF]
            b = acc[:, hh * RET_DK + ROPE_HALF:(hh + 1) * RET_DK]
            o_ref[hh, rs, :ROPE_HALF] = ((a * cos - b * sin) * scale).astype(BF16)
            o_ref[hh, rs, ROPE_HALF:] = ((a * sin + b * cos) * scale).astype(BF16)


def _ret_in(xb, w, col0, n_cols, width, tm, parts, rope=None, tn=1024):
    m, d = xb.shape
    assert tn % width == 0 and (not rope or width == RET_DK)
    in_specs = [pl.BlockSpec((tm, d), lambda j, i: (i, 0)),
                pl.BlockSpec((None, d, tn), lambda j, i: (0, 0, col0 // tn + j))]
    args = [xb, w]
    if rope:
        cos, sin, n_prompt_rows, seq = rope
        assert cos.shape[0] == seq + tm and n_prompt_rows % tm == 0
        per_seq = seq // tm
        table_tile = lambda j, i: (jnp.where(i < n_prompt_rows // tm, i % per_seq, per_seq), 0)
        in_specs += [pl.BlockSpec((tm, ROPE_HALF), table_tile)] * 2
        args += [cos, sin]
    return pl.pallas_call(
        functools.partial(_ret_in_kernel, tn=tn, rope=bool(rope), parts=parts),
        grid=(n_cols // tn, m // tm),
        in_specs=in_specs,
        out_specs=pl.BlockSpec((tn // width, tm, width), lambda j, i: (j, i, 0)),
        out_shape=jax.ShapeDtypeStruct((n_cols // width, m, width), BF16),
        scratch_shapes=[pltpu.VMEM((d, tn), BF16)],
        compiler_params=_params(("arbitrary", "arbitrary"), 60),
        name="ret_in_qk" if rope else "ret_in_vg",
    )(*args)


def _decay_tables(head, blk):
    def log_gamma(shape):
        hv = jnp.full(shape, head, dtype=jnp.int32).astype(F32)
        return jnp.log1p(-jnp.exp2(-5.0 - hv))

    row = lax.broadcasted_iota(jnp.int32, (blk, blk), 0).astype(F32)
    col = lax.broadcasted_iota(jnp.int32, (blk, blk), 1).astype(F32)
    diff = row - col
    decay = jnp.where(diff >= 0, jnp.exp(log_gamma((blk, blk)) * jnp.maximum(diff, 0.0)), 0.0)
    idx_v = lax.broadcasted_iota(jnp.int32, (blk, RET_DV), 0).astype(F32)
    idx_k = lax.broadcasted_iota(jnp.int32, (blk, RET_DK), 0).astype(F32)
    inner = jnp.exp(log_gamma((blk, RET_DV)) * (idx_v + 1.0))
    zeta = jnp.exp(log_gamma((blk, RET_DK)) * (blk - 1.0 - idx_k))
    blk_decay = jnp.exp(log_gamma((1, RET_DV)) * blk)
    return decay, inner, zeta, blk_decay


def _ret_block(qb, kb, vb, gb, state, tables):
    decay, inner, zeta, blk_decay = tables
    scores = lax.dot_general(qb, kb, (((1,), (1,)), ((), ())),
                             preferred_element_type=F32) * decay
    intra = jnp.dot(scores.astype(BF16), vb, preferred_element_type=F32)
    cross = jnp.dot(qb, state.astype(BF16), preferred_element_type=F32) * inner
    kz = (kb.astype(F32) * zeta).T.astype(BF16)
    new_state = blk_decay * state + jnp.dot(kz, vb, preferred_element_type=F32)
    o = intra + cross
    mu = jnp.mean(o, axis=-1, keepdims=True)
    oc = o - mu
    var = jnp.mean(oc * oc, axis=-1, keepdims=True)
    on = oc * lax.rsqrt(var + GN_EPS)
    y = (jax.nn.silu(gb.astype(F32)) * on).astype(BF16)
    return y, new_state


def _ret_kernel(q_ref, k_ref, v_ref, g_ref, qs_ref, ks_ref, vs_ref, gs_ref, s0_ref,
                y_ref, ys_ref, sp_ref, ss_ref, *, blk, n_blk, dec_seq, streams):
    tables = _decay_tables(pl.program_id(0) % RET_HEADS, blk)
    state = jnp.zeros((RET_DK, RET_DV), F32)
    for t in range(n_blk):
        rows = slice(t * blk, (t + 1) * blk)
        y, state = _ret_block(q_ref[rows, :], k_ref[rows, :], v_ref[rows, :],
                              g_ref[rows, :], state, tables)
        y_ref[rows, :] = y
    sp_ref[0, 0] = state
    for i in range(streams):
        rows = slice(i * dec_seq, (i + 1) * dec_seq)
        for h in range(RET_HEADS):
            cv = slice(h * RET_DV, (h + 1) * RET_DV)
            y, new_state = _ret_block(qs_ref[h, rows, :], ks_ref[h, rows, :], vs_ref[h, rows, :],
                                      gs_ref[h, rows, :], s0_ref[0, i, h], _decay_tables(h, dec_seq))
            ys_ref[rows, cv] = y
            ss_ref[i, h] = new_state


def _retention(qk, vg, state0, batch, seq, dec_batch, dec_seq):
    blk = min(seq, RET_CHUNK)
    steps = batch * RET_HEADS
    assert dec_seq <= RET_BLOCK and dec_batch % steps == 0 and (batch * seq) % dec_seq == 0
    streams = dec_batch // steps
    srows = streams * dec_seq
    r_off = batch * seq // srows
    bh = lambda i: (i // RET_HEADS, i % RET_HEADS)
    state_shape = (RET_HEADS, RET_DK, RET_DV)
    return pl.pallas_call(
        functools.partial(_ret_kernel, blk=blk, n_blk=seq // blk, dec_seq=dec_seq,
                          streams=streams),
        grid=(steps,),
        in_specs=[pl.BlockSpec((None, seq, RET_DK), lambda i: (i % RET_HEADS, i // RET_HEADS, 0)),
                  pl.BlockSpec((None, seq, RET_DK),
                               lambda i: (RET_HEADS + i % RET_HEADS, i // RET_HEADS, 0)),
                  pl.BlockSpec((None, seq, RET_DV), lambda i: (i % RET_HEADS, i // RET_HEADS, 0)),
                  pl.BlockSpec((None, seq, RET_DV),
                               lambda i: (RET_HEADS + i % RET_HEADS, i // RET_HEADS, 0)),
                  pl.BlockSpec((RET_HEADS, srows, RET_DK), lambda i: (0, r_off + i, 0)),
                  pl.BlockSpec((RET_HEADS, srows, RET_DK), lambda i: (1, r_off + i, 0)),
                  pl.BlockSpec((RET_HEADS, srows, RET_DV), lambda i: (0, r_off + i, 0)),
                  pl.BlockSpec((RET_HEADS, srows, RET_DV), lambda i: (1, r_off + i, 0)),
                  pl.BlockSpec((1, streams) + state_shape, lambda i: (0, i, 0, 0, 0))],
        out_specs=[pl.BlockSpec((seq, RET_DV), lambda i: bh(i)),
                   pl.BlockSpec((srows, RET_V), lambda i: (i, 0)),
                   pl.BlockSpec((1, 1, RET_DK, RET_DV), lambda i: bh(i) + (0, 0)),
                   pl.BlockSpec((streams,) + state_shape, lambda i: (i, 0, 0, 0))],
        out_shape=[jax.ShapeDtypeStruct((batch * seq, RET_V), BF16),
                   jax.ShapeDtypeStruct((dec_batch * dec_seq, RET_V), BF16),
                   jax.ShapeDtypeStruct((batch, RET_HEADS, RET_DK, RET_DV), F32),
                   jax.ShapeDtypeStruct((dec_batch,) + state_shape, F32)],
        compiler_params=_params(("arbitrary",), 56),
        name="retention",
    )(qk, qk, vg, vg, qk, qk, vg, vg, state0)


def kernel(x_prompt, x_sample, state_ret, w_a_in, a_ln_g, a_ln_b, a_ws, a_bs, w_a_out,
           w_b_in, w_b_out, w_ffn_in, w_ffn_out, ln_mix_g, ln_mix_b, ln_ffn_g, ln_ffn_b):
    batch, seq, d = x_prompt.shape
    dec_batch, dec_seq, _ = x_sample.shape
    n_p = batch * seq
    n_s = dec_batch * dec_seq
    assert d == D_MODEL and seq % GMLP_CHUNK == 0 and GMLP_CHUNK % dec_seq == 0
    assert w_a_in.shape[0] == 1 and w_b_in.shape[0] == 1 and state_ret.shape[0] == 1

    reps = GMLP_CHUNK // dec_seq
    expand = lambda b: jnp.repeat(b.T, GMLP_GROUP_DIM, axis=1)
    wmix = jnp.stack([a_ws[0], jnp.tile(a_ws[0][:, :dec_seq, :dec_seq], (1, reps, reps))])
    bias = jnp.stack([expand(a_bs[0]), expand(jnp.tile(a_bs[0][:, :dec_seq], (1, reps)))])
    x1, x1b, v_sample = _gmlp_layer(x_prompt.reshape(n_p, d), x_sample.reshape(n_s, d), w_a_in,
                                    w_a_out, a_ln_g, a_ln_b, wmix, bias, ln_mix_g, ln_mix_b, dec_seq)
    act = _ffn_in(x1b, w_ffn_in, 0)
    x2, x2b = _proj_ln(act, w_ffn_out, 0, x1, ln_ffn_g, ln_ffn_b, 0, chunk_rows=512)

    cos, sin = _rope_tables(seq, dec_seq, rows=1024)
    qk = _ret_in(x2b, w_b_in, 0, 2 * RET_QK, RET_DK, tm=1024, parts=1, rope=(cos, sin, n_p, seq))
    vg = _ret_in(x2b, w_b_in, 2 * RET_QK, 2 * RET_V, RET_DV, tm=2304, parts=2)
    y_p, y_s, state_p, state_s = _retention(qk, vg, state_ret, batch, seq, dec_batch, dec_seq)
    x3, x3b = _proj_ln((y_p, y_s), w_b_out, 0, x2, ln_mix_g, ln_mix_b, 1, chunk_rows=512)
    act = _ffn_in(x3b, w_ffn_in, 1)
    y_prompt, y_sample = _proj_ln(act, w_ffn_out, 1, x3, ln_ffn_g, ln_ffn_b, 1, chunk_rows=512,
                                  split_rows=n_p)

    return (y_prompt.reshape(batch, seq, d), y_sample.reshape(dec_batch, dec_seq, d),
            state_p[None], state_s[None], v_sample.reshape(1, dec_batch, dec_seq, d))
```

```python
import functools
import math

import jax
import jax.numpy as jnp
import numpy as np
from jax import lax
from jax.experimental import pallas as pl
from jax.experimental.pallas import tpu as pltpu

D_MODEL = 2048
DEPTH = 2
PAST_LEN = 4096
RET_BLOCK = 64
RET_CHUNK = 256
GMLP_CHUNK = 128
GMLP_GROUPS = 8
GMLP_GROUP_DIM = D_MODEL // GMLP_GROUPS
RET_HEADS = 8
RET_DK = D_MODEL // RET_HEADS
RET_DV = 2 * RET_DK
RET_QK = RET_HEADS * RET_DK
RET_V = RET_HEADS * RET_DV
ROPE_BASE = 10000.0
ROPE_HALF = RET_DK // 2
ALPHA = (2 * DEPTH) ** 0.25
LN_EPS = 1e-5
GN_EPS = 1e-6

F32 = jnp.float32
BF16 = jnp.bfloat16
MIB = 1024 * 1024


def _params(semantics, vmem_mib):
    return pltpu.CompilerParams(dimension_semantics=semantics,
                                vmem_limit_bytes=vmem_mib * MIB)


def _layer_norm(x, g, b):
    mu = jnp.mean(x, axis=-1, keepdims=True)
    xc = x - mu
    var = jnp.mean(xc * xc, axis=-1, keepdims=True)
    return xc * lax.rsqrt(var + LN_EPS) * g + b


def _gelu(x):
    return 0.5 * x * (1.0 + lax.erf(x * np.float32(math.sqrt(0.5))))


def _stage_weight(step, w_ref, wb_ref, n_chunks):
    rows = w_ref.shape[0]

    @pl.when(step < n_chunks)
    def _():
        r0 = pl.multiple_of(step * rows, rows)
        wb_ref[pl.ds(r0, rows), :] = w_ref[...].astype(BF16)


def _whole(a):
    return pl.BlockSpec(a.shape, lambda s: (0,) * a.ndim)


def _chunk_spec(layer, rows, cols, n_chunks):
    return pl.BlockSpec((None, rows, cols),
                        lambda s: (layer, jnp.minimum(s, n_chunks - 1), 0))


def _gmlp_kernel(xp_ref, xs_ref, win_ref, wout_ref, lng_ref, lnb_ref, wmix_ref, bias_ref, g_ref,
                 b_ref, x1_ref, x1b_ref, vn_ref, win_b, wout_b, u_ref, v_ref, s0, s1, xr0, xr1,
                 *, tm, n_stage, n_tiles, n_prompt_tiles, dec_seq):
    step = pl.program_id(0)
    _stage_weight(step, win_ref, win_b, n_stage)
    _stage_weight(step, wout_ref, wout_b, n_stage)
    t = step - n_stage
    d = D_MODEL

    def gate_in(xr_ref, sample):
        xf = (xs_ref if sample else xp_ref)[...]
        xr_ref[...] = xf
        xb = xf.astype(BF16)
        u_ref[...] = _gelu(jnp.dot(xb, win_b[:, :d], preferred_element_type=F32))
        v = _layer_norm(_gelu(jnp.dot(xb, win_b[:, d:], preferred_element_type=F32)),
                        lng_ref[...], lnb_ref[...])
        if sample:
            vn_ref[...] = v
        v_ref[...] = v.astype(BF16)

    def gate_mix(s_ref, sample):
        row = lax.broadcasted_iota(jnp.int32, (GMLP_CHUNK, GMLP_CHUNK), 0)
        col = lax.broadcasted_iota(jnp.int32, (GMLP_CHUNK, GMLP_CHUNK), 1)
        mask = col <= row
        if sample:
            mask = mask & (((row ^ col) & ~(dec_seq - 1)) == 0)
        which = 1 if sample else 0
        for g in range(GMLP_GROUPS):
            w_g = jnp.where(mask, wmix_ref[which, g], 0.0).astype(BF16)
            cs = slice(g * GMLP_GROUP_DIM, (g + 1) * GMLP_GROUP_DIM)
            for seg in range(tm // GMLP_CHUNK):
                rs = slice(seg * GMLP_CHUNK, (seg + 1) * GMLP_CHUNK)
                mixed = jnp.dot(w_g, v_ref[rs, cs], preferred_element_type=F32)
                mixed = mixed + bias_ref[which, :, cs]
                s_ref[rs, cs] = (u_ref[rs, cs] * mixed).astype(BF16)

    def project(s_ref):
        return jnp.dot(s_ref[...], wout_b[...], preferred_element_type=F32)

    def finish(h, xr_ref):
        x1 = _layer_norm(ALPHA * xr_ref[...] + h, g_ref[0:1, :], b_ref[0:1, :])
        x1_ref[...] = x1
        x1b_ref[...] = x1.astype(BF16)

    @pl.when(t == 0)
    def _():
        gate_in(xr0, False)
        gate_mix(s0, False)

    for parity, (cur, prev) in enumerate((((s0, xr0), (s1, xr1)), ((s1, xr1), (s0, xr0)))):
        for sample in (False, True):
            cond = ((t & 1) == parity) & (t >= 1) & (t < n_tiles)
            cond &= (t >= n_prompt_tiles) if sample else (t < n_prompt_tiles)

            @pl.when(cond)
            def _(cur=cur, prev=prev, sample=sample):
                gate_in(cur[1], sample)
                h = project(prev[0])
                gate_mix(cur[0], sample)
                finish(h, prev[1])

    @pl.when(t == n_tiles)
    def _():
        last = (s0, xr0) if (n_tiles - 1) % 2 == 0 else (s1, xr1)
        finish(project(last[0]), last[1])


def _gmlp_layer(xp, xs, w_in, w_out, ln_g, ln_b, wmix, bias, mix_g, mix_b, dec_seq,
                tm=256, n_stage=32):
    (n_p, d), n_s = xp.shape, xs.shape[0]
    n_prompt_tiles, n_sample_tiles = n_p // tm, n_s // tm
    n_tiles = n_prompt_tiles + n_sample_tiles
    assert n_p % tm == 0 and n_s % tm == 0 and n_sample_tiles >= 1
    clip = lambda v, n: jnp.clip(v, 0, n - 1)
    sample_tile = lambda s: (clip(s - n_stage - n_prompt_tiles, n_sample_tiles), 0)
    out_tile = lambda s: (clip(s - n_stage - 1, n_tiles), 0)
    once = pl.Buffered(1)
    in_specs = [pl.BlockSpec((tm, d), lambda s: (clip(s - n_stage, n_prompt_tiles), 0)),
                pl.BlockSpec((tm, d), sample_tile),
                _chunk_spec(0, d // n_stage, 2 * d, n_stage),
                _chunk_spec(0, d // n_stage, d, n_stage),
                _whole(ln_g), _whole(ln_b),
                pl.BlockSpec((2, GMLP_GROUPS, GMLP_CHUNK, GMLP_CHUNK), lambda s: (0, 0, 0, 0),
                             pipeline_mode=once),
                pl.BlockSpec((2, GMLP_CHUNK, d), lambda s: (0, 0, 0), pipeline_mode=once),
                _whole(mix_g), _whole(mix_b)]
    return pl.pallas_call(
        functools.partial(_gmlp_kernel, tm=tm, n_stage=n_stage, n_tiles=n_tiles,
                          n_prompt_tiles=n_prompt_tiles, dec_seq=dec_seq),
        grid=(n_stage + n_tiles + 1,),
        in_specs=in_specs,
        out_specs=[pl.BlockSpec((tm, d), out_tile), pl.BlockSpec((tm, d), out_tile),
                   pl.BlockSpec((tm, d), sample_tile)],
        out_shape=[jax.ShapeDtypeStruct((n_p + n_s, d), F32),
                   jax.ShapeDtypeStruct((n_p + n_s, d), BF16),
                   jax.ShapeDtypeStruct((n_s, d), F32)],
        scratch_shapes=[pltpu.VMEM((d, 2 * d), BF16), pltpu.VMEM((d, d), BF16),
                        pltpu.VMEM((tm, d), F32), pltpu.VMEM((tm, d), BF16),
                        pltpu.VMEM((tm, d), BF16), pltpu.VMEM((tm, d), BF16),
                        pltpu.VMEM((tm, d), F32), pltpu.VMEM((tm, d), F32)],
        compiler_params=_params(("arbitrary",), 62),
        name="gmlp",
    )(xp, xs, w_in, w_out, ln_g, ln_b, wmix, bias, mix_g, mix_b)


def _ffn_in_kernel(x_ref, wg_ref, wu_ref, o_ref, wg_b, wu_b, *, parts):
    @pl.when(pl.program_id(1) == 0)
    def _():
        wg_b[...] = wg_ref[...].astype(BF16)
        wu_b[...] = wu_ref[...].astype(BF16)

    rows = x_ref.shape[0] // parts
    for p in range(parts):
        rs = slice(p * rows, (p + 1) * rows)
        xb = x_ref[rs, :]
        gate = jnp.dot(xb, wg_b[...], preferred_element_type=F32)
        up = jnp.dot(xb, wu_b[...], preferred_element_type=F32)
        o_ref[rs, :] = (jax.nn.silu(gate) * up).astype(BF16)


def _ffn_in(xb, w_in, layer, tm=2304, th=512, parts=2):
    m, d = xb.shape
    hidden = w_in.shape[2] // 2
    nh = hidden // th
    return pl.pallas_call(
        functools.partial(_ffn_in_kernel, parts=parts),
        grid=(nh, m // tm),
        in_specs=[pl.BlockSpec((tm, d), lambda j, i: (i, 0)),
                  pl.BlockSpec((None, d, th), lambda j, i: (layer, 0, j)),
                  pl.BlockSpec((None, d, th), lambda j, i: (layer, 0, j + nh))],
        out_specs=pl.BlockSpec((tm, th), lambda j, i: (i, j)),
        out_shape=jax.ShapeDtypeStruct((m, hidden), BF16),
        scratch_shapes=[pltpu.VMEM((d, th), BF16), pltpu.VMEM((d, th), BF16)],
        compiler_params=_params(("arbitrary", "arbitrary"), 60),
        name="ffn_in",
    )(xb, w_in, w_in)


def _proj_ln_kernel(*refs, n_stage, n_tiles, src_split, out_split, ln_row):
    n_a = 1 if src_split is None else 2
    a_refs, (w_ref, x_ref, g_ref, b_ref), rest = refs[:n_a], refs[n_a:n_a + 4], refs[n_a + 4:]
    outs, (w_b, h0, h1) = rest[:-3], rest[-3:]
    step = pl.program_id(0)
    t = step - n_stage + 1

    rows = w_ref.shape[0]
    for c in range(n_stage):
        @pl.when(step == c)
        def _(c=c):
            ks = slice(c * rows, (c + 1) * rows)
            w_c = w_ref[...].astype(BF16)
            w_b[ks, :] = w_c
            part = jnp.dot(a_refs[0][:, ks], w_c, preferred_element_type=F32)
            if c == 0:
                h0[...] = part
            else:
                h0[...] += part

    def matmul(dst, second_src):
        a_ref = a_refs[1 if second_src else 0]
        dst[...] = jnp.dot(a_ref[...], w_b[...], preferred_element_type=F32)

    def epilogue(src, second_out):
        ln = slice(ln_row, ln_row + 1)
        x = _layer_norm(ALPHA * x_ref[...] + src[...], g_ref[ln, :], b_ref[ln, :])
        if out_split is None:
            outs[0][...] = x
            outs[1][...] = x.astype(BF16)
        else:
            outs[1 if second_out else 0][...] = x

    for parity, (dst, src) in enumerate(((h0, h1), (h1, h0))):
        for second_src in ((False, True) if src_split is not None else (False,)):
            for second_out in ((False, True) if out_split is not None else (False,)):
                cond = ((t & 1) == parity) & (t >= 1) & (t < n_tiles)
                if src_split is not None:
                    cond &= (t >= src_split) if second_src else (t < src_split)
                if out_split is not None:
                    cond &= (t - 1 >= out_split) if second_out else (t - 1 < out_split)

                @pl.when(cond)
                def _(dst=dst, src=src, second_src=second_src, second_out=second_out):
                    matmul(dst, second_src)
                    epilogue(src, second_out)

    @pl.when(t == n_tiles)
    def _():
        epilogue(h0 if (n_tiles - 1) % 2 == 0 else h1, out_split is not None)


def _proj_ln(a, w, layer, x, ln_g, ln_b, ln_row, chunk_rows, split_rows=None, tm=256):
    a_parts = a if isinstance(a, tuple) else (a,)
    kdim = a_parts[0].shape[1]
    m = sum(p.shape[0] for p in a_parts)
    d = w.shape[2]
    n_stage = kdim // chunk_rows
    n_tiles = m // tm
    clip = lambda v, n: jnp.clip(v, 0, n - 1)
    ln_tile = lambda s: (clip(s - n_stage, n_tiles), 0)
    if len(a_parts) == 1:
        src_split = None
        a_specs = [pl.BlockSpec((tm, kdim), lambda s: (clip(s - n_stage + 1, n_tiles), 0))]
    else:
        src_split = a_parts[0].shape[0] // tm
        assert 1 <= src_split < n_tiles and a_parts[0].shape[0] % tm == 0
        a_specs = [pl.BlockSpec((tm, kdim), lambda s: (clip(s - n_stage + 1, src_split), 0)),
                   pl.BlockSpec((tm, kdim), lambda s: (
                       clip(s - n_stage + 1 - src_split, n_tiles - src_split), 0))]
    in_specs = a_specs + [_chunk_spec(layer, chunk_rows, d, n_stage),
                          pl.BlockSpec((tm, d), ln_tile),
                          _whole(ln_g), _whole(ln_b)]
    if split_rows is None:
        out_split = None
        out_specs = [pl.BlockSpec((tm, d), ln_tile), pl.BlockSpec((tm, d), ln_tile)]
        out_shape = [jax.ShapeDtypeStruct((m, d), F32), jax.ShapeDtypeStruct((m, d), BF16)]
    else:
        out_split = split_rows // tm
        assert 1 <= out_split < n_tiles and split_rows % tm == 0
        out_specs = [
            pl.BlockSpec((tm, d), lambda s: (clip(s - n_stage, out_split), 0)),
            pl.BlockSpec((tm, d), lambda s: (clip(s - n_stage - out_split, n_tiles - out_split), 0))]
        out_shape = [jax.ShapeDtypeStruct((split_rows, d), F32),
                     jax.ShapeDtypeStruct((m - split_rows, d), F32)]
    return pl.pallas_call(
        functools.partial(_proj_ln_kernel, n_stage=n_stage, n_tiles=n_tiles,
                          src_split=src_split, out_split=out_split, ln_row=ln_row),
        grid=(n_stage + n_tiles,),
        in_specs=in_specs,
        out_specs=out_specs,
        out_shape=out_shape,
        scratch_shapes=[pltpu.VMEM((kdim, d), BF16), pltpu.VMEM((tm, d), F32),
                        pltpu.VMEM((tm, d), F32)],
        compiler_params=_params(("arbitrary",), 60),
        name="proj_ln_split" if split_rows else "proj_ln",
    )(*a_parts, w, x, ln_g, ln_b)


def _rope_table_kernel(cos_ref, sin_ref, *, rows, seq, dec_seq):
    r = lax.broadcasted_iota(jnp.int32, (rows, ROPE_HALF), 0) + pl.program_id(0) * rows
    j = lax.broadcasted_iota(jnp.int32, (rows, ROPE_HALF), 1)
    pos = jnp.where(r < seq, r, PAST_LEN + ((r - seq) & (dec_seq - 1)))
    inv = jnp.power(np.float32(ROPE_BASE), -j.astype(F32) / ROPE_HALF)
    ang = pos.astype(F32) * inv
    cos_ref[...] = jnp.cos(ang)
    sin_ref[...] = jnp.sin(ang)


def _rope_tables(seq, dec_seq, rows):
    assert seq % rows == 0 and rows % dec_seq == 0 and dec_seq & (dec_seq - 1) == 0
    spec = pl.BlockSpec((rows, ROPE_HALF), lambda i: (i, 0))
    return pl.pallas_call(
        functools.partial(_rope_table_kernel, rows=rows, seq=seq, dec_seq=dec_seq),
        grid=(seq // rows + 1,),
        out_specs=[spec, spec],
        out_shape=[jax.ShapeDtypeStruct((seq + rows, ROPE_HALF), F32)] * 2,
        compiler_params=_params(("parallel",), 16),
        name="rope_tables",
    )()


def _ret_in_kernel(x_ref, w_ref, *rest, tn, rope, parts):
    if rope:
        cos_ref, sin_ref, o_ref, w_b = rest
    else:
        o_ref, w_b = rest

    @pl.when(pl.program_id(1) == 0)
    def _():
        w_b[...] = w_ref[...].astype(BF16)

    rows = x_ref.shape[0] // parts
    heads, _, width = o_ref.shape
    for p in range(parts):
        rs = slice(p * rows, (p + 1) * rows)
        acc = jnp.dot(x_ref[rs, :], w_b[...], preferred_element_type=F32)
        if not rope:
            for hh in range(heads):
                o_ref[hh, rs, :] = acc[:, hh * width:(hh + 1) * width].astype(BF16)
            continue
        scale = jnp.where(pl.program_id(0) < RET_QK // tn, np.float32(RET_DK ** -0.5),
                          np.float32(1.0))
        cos = cos_ref[rs, :]
        sin = sin_ref[rs, :]
        for hh in range(heads):
            a = acc[:, hh * RET_DK:hh * RET_DK + ROPE_HALF]
            b = acc[:, hh * RET_DK + ROPE_HALF:(hh + 1) * RET_DK]
            o_ref[hh, rs, :ROPE_HALF] = ((a * cos - b * sin) * scale).astype(BF16)
            o_ref[hh, rs, ROPE_HALF:] = ((a * sin + b * cos) * scale).astype(BF16)


def _ret_in(xb, w, col0, n_cols, width, tm, parts, rope=None, tn=1024):
    m, d = xb.shape
    assert tn % width == 0 and (not rope or width == RET_DK)
    in_specs = [pl.BlockSpec((tm, d), lambda j, i: (i, 0)),
                pl.BlockSpec((None, d, tn), lambda j, i: (0, 0, col0 // tn + j))]
    args = [xb, w]
    if rope:
        cos, sin, n_prompt_rows, seq = rope
        assert cos.shape[0] == seq + tm and n_prompt_rows % tm == 0
        per_seq = seq // tm
        table_tile = lambda j, i: (jnp.where(i < n_prompt_rows // tm, i % per_seq, per_seq), 0)
        in_specs += [pl.BlockSpec((tm, ROPE_HALF), table_tile)] * 2
        args += [cos, sin]
    return pl.pallas_call(
        functools.partial(_ret_in_kernel, tn=tn, rope=bool(rope), parts=parts),
        grid=(n_cols // tn, m // tm),
        in_specs=in_specs,
        out_specs=pl.BlockSpec((tn // width, tm, width), lambda j, i: (j, i, 0)),
        out_shape=jax.ShapeDtypeStruct((n_cols // width, m, width), BF16),
        scratch_shapes=[pltpu.VMEM((d, tn), BF16)],
        compiler_params=_params(("arbitrary", "arbitrary"), 60),
        name="ret_in_qk" if rope else "ret_in_vg",
    )(*args)


def _decay_tables(head, blk):
    def log_gamma(shape):
        hv = jnp.full(shape, head, dtype=jnp.int32).astype(F32)
        return jnp.log1p(-jnp.exp2(-5.0 - hv))

    row = lax.broadcasted_iota(jnp.int32, (blk, blk), 0).astype(F32)
    col = lax.broadcasted_iota(jnp.int32, (blk, blk), 1).astype(F32)
    diff = row - col
    decay = jnp.where(diff >= 0, jnp.exp(log_gamma((blk, blk)) * jnp.maximum(diff, 0.0)), 0.0)
    idx_v = lax.broadcasted_iota(jnp.int32, (blk, RET_DV), 0).astype(F32)
    idx_k = lax.broadcasted_iota(jnp.int32, (blk, RET_DK), 0).astype(F32)
    inner = jnp.exp(log_gamma((blk, RET_DV)) * (idx_v + 1.0))
    zeta = jnp.exp(log_gamma((blk, RET_DK)) * (blk - 1.0 - idx_k))
    blk_decay = jnp.exp(log_gamma((1, RET_DV)) * blk)
    return decay, inner, zeta, blk_decay


def _ret_block(qb, kb, vb, gb, state, tables):
    decay, inner, zeta, blk_decay = tables
    scores = lax.dot_general(qb, kb, (((1,), (1,)), ((), ())),
                             preferred_element_type=F32) * decay
    intra = jnp.dot(scores.astype(BF16), vb, preferred_element_type=F32)
    cross = jnp.dot(qb, state.astype(BF16), preferred_element_type=F32) * inner
    kz = (kb.astype(F32) * zeta).T.astype(BF16)
    new_state = blk_decay * state + jnp.dot(kz, vb, preferred_element_type=F32)
    o = intra + cross
    mu = jnp.mean(o, axis=-1, keepdims=True)
    oc = o - mu
    var = jnp.mean(oc * oc, axis=-1, keepdims=True)
    on = oc * lax.rsqrt(var + GN_EPS)
    y = (jax.nn.silu(gb.astype(F32)) * on).astype(BF16)
    return y, new_state


def _ret_kernel(q_ref, k_ref, v_ref, g_ref, qs_ref, ks_ref, vs_ref, gs_ref, s0_ref,
                y_ref, ys_ref, sp_ref, ss_ref, *, blk, n_blk, dec_seq, streams):
    tables = _decay_tables(pl.program_id(0) % RET_HEADS, blk)
    state = jnp.zeros((RET_DK, RET_DV), F32)
    for t in range(n_blk):
        rows = slice(t * blk, (t + 1) * blk)
        y, state = _ret_block(q_ref[rows, :], k_ref[rows, :], v_ref[rows, :],
                              g_ref[rows, :], state, tables)
        y_ref[rows, :] = y
    sp_ref[0, 0] = state
    for i in range(streams):
        rows = slice(i * dec_seq, (i + 1) * dec_seq)
        for h in range(RET_HEADS):
            cv = slice(h * RET_DV, (h + 1) * RET_DV)
            y, new_state = _ret_block(qs_ref[h, rows, :], ks_ref[h, rows, :], vs_ref[h, rows, :],
                                      gs_ref[h, rows, :], s0_ref[0, i, h], _decay_tables(h, dec_seq))
            ys_ref[rows, cv] = y
            ss_ref[i, h] = new_state


def _retention(qk, vg, state0, batch, seq, dec_batch, dec_seq):
    blk = min(seq, RET_CHUNK)
    steps = batch * RET_HEADS
    assert dec_seq <= RET_BLOCK and dec_batch % steps == 0 and (batch * seq) % dec_seq == 0
    streams = dec_batch // steps
    srows = streams * dec_seq
    r_off = batch * seq // srows
    bh = lambda i: (i // RET_HEADS, i % RET_HEADS)
    state_shape = (RET_HEADS, RET_DK, RET_DV)
    return pl.pallas_call(
        functools.partial(_ret_kernel, blk=blk, n_blk=seq // blk, dec_seq=dec_seq,
                          streams=streams),
        grid=(steps,),
        in_specs=[pl.BlockSpec((None, seq, RET_DK), lambda i: (i % RET_HEADS, i // RET_HEADS, 0)),
                  pl.BlockSpec((None, seq, RET_DK),
                               lambda i: (RET_HEADS + i % RET_HEADS, i // RET_HEADS, 0)),
                  pl.BlockSpec((None, seq, RET_DV), lambda i: (i % RET_HEADS, i // RET_HEADS, 0)),
                  pl.BlockSpec((None, seq, RET_DV),
                               lambda i: (RET_HEADS + i % RET_HEADS, i // RET_HEADS, 0)),
                  pl.BlockSpec((RET_HEADS, srows, RET_DK), lambda i: (0, r_off + i, 0)),
                  pl.BlockSpec((RET_HEADS, srows, RET_DK), lambda i: (1, r_off + i, 0)),
                  pl.BlockSpec((RET_HEADS, srows, RET_DV), lambda i: (0, r_off + i, 0)),
                  pl.BlockSpec((RET_HEADS, srows, RET_DV), lambda i: (1, r_off + i, 0)),
                  pl.BlockSpec((1, streams) + state_shape, lambda i: (0, i, 0, 0, 0))],
        out_specs=[pl.BlockSpec((seq, RET_DV), lambda i: bh(i)),
                   pl.BlockSpec((srows, RET_V), lambda i: (i, 0)),
                   pl.BlockSpec((1, 1, RET_DK, RET_DV), lambda i: bh(i) + (0, 0)),
                   pl.BlockSpec((streams,) + state_shape, lambda i: (i, 0, 0, 0))],
        out_shape=[jax.ShapeDtypeStruct((batch * seq, RET_V), BF16),
                   jax.ShapeDtypeStruct((dec_batch * dec_seq, RET_V), BF16),
                   jax.ShapeDtypeStruct((batch, RET_HEADS, RET_DK, RET_DV), F32),
                   jax.ShapeDtypeStruct((dec_batch,) + state_shape, F32)],
        compiler_params=_params(("arbitrary",), 56),
        name="retention",
    )(qk, qk, vg, vg, qk, qk, vg, vg, state0)


def kernel(x_prompt, x_sample, state_ret, w_a_in, a_ln_g, a_ln_b, a_ws, a_bs, w_a_out,
           w_b_in, w_b_out, w_ffn_in, w_ffn_out, ln_mix_g, ln_mix_b, ln_ffn_g, ln_ffn_b):
    batch, seq, d = x_prompt.shape
    dec_batch, dec_seq, _ = x_sample.shape
    n_p = batch * seq
    n_s = dec_batch * dec_seq
    assert d == D_MODEL and seq % GMLP_CHUNK == 0 and GMLP_CHUNK % dec_seq == 0
    assert w_a_in.shape[0] == 1 and w_b_in.shape[0] == 1 and state_ret.shape[0] == 1

    reps = GMLP_CHUNK // dec_seq
    expand = lambda b: jnp.repeat(b.T, GMLP_GROUP_DIM, axis=1)
    wmix = jnp.stack([a_ws[0], jnp.tile(a_ws[0][:, :dec_seq, :dec_seq], (1, reps, reps))])
    bias = jnp.stack([expand(a_bs[0]), expand(jnp.tile(a_bs[0][:, :dec_seq], (1, reps)))])
    x1, x1b, v_sample = _gmlp_layer(x_prompt.reshape(n_p, d), x_sample.reshape(n_s, d), w_a_in,
                                    w_a_out, a_ln_g, a_ln_b, wmix, bias, ln_mix_g, ln_mix_b, dec_seq)
    act = _ffn_in(x1b, w_ffn_in, 0)
    x2, x2b = _proj_ln(act, w_ffn_out, 0, x1, ln_ffn_g, ln_ffn_b, 0, chunk_rows=512)

    cos, sin = _rope_tables(seq, dec_seq, rows=1024)
    qk = _ret_in(x2b, w_b_in, 0, 2 * RET_QK, RET_DK, tm=1024, parts=1, rope=(cos, sin, n_p, seq))
    vg = _ret_in(x2b, w_b_in, 2 * RET_QK, 2 * RET_V, RET_DV, tm=2304, parts=2)
    y_p, y_s, state_p, state_s = _retention(qk, vg, state_ret, batch, seq, dec_batch, dec_seq)
    x3, x3b = _proj_ln((y_p, y_s), w_b_out, 0, x2, ln_mix_g, ln_mix_b, 1, chunk_rows=512)
    act = _ffn_in(x3b, w_ffn_in, 1)
    y_prompt, y_sample = _proj_ln(act, w_ffn_out, 1, x3, ln_ffn_g, ln_ffn_b, 1, chunk_rows=512,
                                  split_rows=n_p)

    return (y_prompt.reshape(batch, seq, d), y_sample.reshape(dec_batch, dec_seq, d),
            state_p[None], state_s[None], v_sample.reshape(1, dec_batch, dec_seq, d))
```

```python
import functools
import math

import jax
import jax.numpy as jnp
import numpy as np
from jax import lax
from jax.experimental import pallas as pl
from jax.experimental.pallas import tpu as pltpu

D_MODEL = 2048
DEPTH = 2
PAST_LEN = 4096
RET_BLOCK = 64
RET_CHUNK = 256
GMLP_CHUNK = 128
GMLP_GROUPS = 8
GMLP_GROUP_DIM = D_MODEL // GMLP_GROUPS
RET_HEADS = 8
RET_DK = D_MODEL // RET_HEADS
RET_DV = 2 * RET_DK
RET_QK = RET_HEADS * RET_DK
RET_V = RET_HEADS * RET_DV
ROPE_BASE = 10000.0
ROPE_HALF = RET_DK // 2
ALPHA = (2 * DEPTH) ** 0.25
LN_EPS = 1e-5
GN_EPS = 1e-6

F32 = jnp.float32
BF16 = jnp.bfloat16
MIB = 1024 * 1024


def _params(semantics, vmem_mib):
    return pltpu.CompilerParams(dimension_semantics=semantics,
                                vmem_limit_bytes=vmem_mib * MIB)


def _layer_norm(x, g, b):
    mu = jnp.mean(x, axis=-1, keepdims=True)
    xc = x - mu
    var = jnp.mean(xc * xc, axis=-1, keepdims=True)
    return xc * lax.rsqrt(var + LN_EPS) * g + b


def _gelu(x):
    return 0.5 * x * (1.0 + lax.erf(x * np.float32(math.sqrt(0.5))))


def _stage_weight(step, w_ref, wb_ref, n_chunks):
    rows = w_ref.shape[0]

    @pl.when(step < n_chunks)
    def _():
        r0 = pl.multiple_of(step * rows, rows)
        wb_ref[pl.ds(r0, rows), :] = w_ref[...].astype(BF16)


def _whole(a):
    return pl.BlockSpec(a.shape, lambda s: (0,) * a.ndim)


def _chunk_spec(layer, rows, cols, n_chunks):
    return pl.BlockSpec((None, rows, cols),
                        lambda s: (layer, jnp.minimum(s, n_chunks - 1), 0))


def _gmlp_in_kernel(xp_ref, xs_ref, w_ref, o_ref, w_b, *, n_prompt_tiles, parts):
    i = pl.program_id(1)

    @pl.when(i == 0)
    def _():
        w_b[...] = w_ref[...].astype(BF16)

    def tile(x_ref):
        rows = x_ref.shape[0] // parts
        for p in range(parts):
            rs = slice(p * rows, (p + 1) * rows)
            acc = jnp.dot(x_ref[rs, :].astype(BF16), w_b[...], preferred_element_type=F32)
            o_ref[rs, :] = _gelu(acc).astype(BF16)

    @pl.when(i < n_prompt_tiles)
    def _():
        tile(xp_ref)

    @pl.when(i >= n_prompt_tiles)
    def _():
        tile(xs_ref)


def _gmlp_in(xp, xs, w_in, tm=1024, tn=1024, parts=2):
    (n_p, d), n_s = xp.shape, xs.shape[0]
    n = w_in.shape[2]
    n_prompt_tiles, n_sample_tiles = n_p // tm, n_s // tm
    assert n_p % tm == 0 and n_s % tm == 0 and n_sample_tiles >= 1
    clip = lambda v, k: jnp.clip(v, 0, k - 1)
    return pl.pallas_call(
        functools.partial(_gmlp_in_kernel, n_prompt_tiles=n_prompt_tiles, parts=parts),
        grid=(n // tn, n_prompt_tiles + n_sample_tiles),
        in_specs=[pl.BlockSpec((tm, d), lambda j, i: (clip(i, n_prompt_tiles), 0)),
                  pl.BlockSpec((tm, d), lambda j, i: (clip(i - n_prompt_tiles, n_sample_tiles), 0),
                               pipeline_mode=pl.Buffered(1 if n_sample_tiles == 1 else 2)),
                  pl.BlockSpec((None, d, tn), lambda j, i: (0, 0, j))],
        out_specs=pl.BlockSpec((tm, tn), lambda j, i: (i, j)),
        out_shape=jax.ShapeDtypeStruct((n_p + n_s, n), BF16),
        scratch_shapes=[pltpu.VMEM((d, tn), BF16)],
        compiler_params=_params(("arbitrary", "arbitrary"), 60),
        name="gmlp_in",
    )(xp, xs, w_in)


def _gmlp_gate_kernel(u_ref, v_ref, xp_ref, xs_ref, wout_ref, lng_ref, lnb_ref, wmix_ref, bias_ref,
                      g_ref, b_ref, x1_ref, x1b_ref, vn_ref, wout_b, vb_ref, s_ref,
                      *, tm, n_stage, n_prompt_tiles, dec_seq):
    step = pl.program_id(0)
    _stage_weight(step, wout_ref, wout_b, n_stage)
    t = step - n_stage

    def tile(x_ref, sample):
        row = lax.broadcasted_iota(jnp.int32, (GMLP_CHUNK, GMLP_CHUNK), 0)
        col = lax.broadcasted_iota(jnp.int32, (GMLP_CHUNK, GMLP_CHUNK), 1)
        mask = col <= row
        if sample:
            mask = mask & (((row ^ col) & ~(dec_seq - 1)) == 0)
        which = 1 if sample else 0
        w_mix = [jnp.where(mask, wmix_ref[which, g], 0.0).astype(BF16) for g in range(GMLP_GROUPS)]
        for seg in range(tm // GMLP_CHUNK):
            rs = slice(seg * GMLP_CHUNK, (seg + 1) * GMLP_CHUNK)
            v = _layer_norm(v_ref[rs, :].astype(F32), lng_ref[...], lnb_ref[...])
            if sample:
                vn_ref[rs, :] = v
            vb_ref[rs, :] = v.astype(BF16)
            for g in range(GMLP_GROUPS):
                cs = slice(g * GMLP_GROUP_DIM, (g + 1) * GMLP_GROUP_DIM)
                mixed = jnp.dot(w_mix[g], vb_ref[rs, cs], preferred_element_type=F32)
                mixed = mixed + bias_ref[which, :, cs]
                s_ref[rs, cs] = (u_ref[rs, cs].astype(F32) * mixed).astype(BF16)
            h = jnp.dot(s_ref[rs, :], wout_b[...], preferred_element_type=F32)
            x1 = _layer_norm(ALPHA * x_ref[rs, :] + h, g_ref[0:1, :], b_ref[0:1, :])
            x1_ref[rs, :] = x1
            x1b_ref[rs, :] = x1.astype(BF16)

    @pl.when((t >= 0) & (t < n_prompt_tiles))
    def _():
        tile(xp_ref, False)

    @pl.when(t >= n_prompt_tiles)
    def _():
        tile(xs_ref, True)


def _gmlp_gate(z, xp, xs, w_out, ln_g, ln_b, wmix, bias, mix_g, mix_b, dec_seq, tm=256, n_stage=8):
    (n_p, d), n_s = xp.shape, xs.shape[0]
    n_prompt_tiles, n_sample_tiles = n_p // tm, n_s // tm
    n_tiles = n_prompt_tiles + n_sample_tiles
    assert n_p % tm == 0 and n_s % tm == 0 and n_sample_tiles >= 1
    clip = lambda v, n: jnp.clip(v, 0, n - 1)
    tile = lambda s: clip(s - n_stage, n_tiles)
    sample_tile = lambda s: (clip(s - n_stage - n_prompt_tiles, n_sample_tiles), 0)
    once = pl.Buffered(1)
    in_specs = [pl.BlockSpec((tm, d), lambda s: (tile(s), 0)),
                pl.BlockSpec((tm, d), lambda s: (tile(s), 1)),
                pl.BlockSpec((tm, d), lambda s: (clip(s - n_stage, n_prompt_tiles), 0)),
                pl.BlockSpec((tm, d), sample_tile),
                _chunk_spec(0, d // n_stage, d, n_stage),
                _whole(ln_g), _whole(ln_b),
                pl.BlockSpec((2, GMLP_GROUPS, GMLP_CHUNK, GMLP_CHUNK), lambda s: (0, 0, 0, 0),
                             pipeline_mode=once),
                pl.BlockSpec((2, GMLP_CHUNK, d), lambda s: (0, 0, 0), pipeline_mode=once),
                _whole(mix_g), _whole(mix_b)]
    out_tile = lambda s: (tile(s), 0)
    return pl.pallas_call(
        functools.partial(_gmlp_gate_kernel, tm=tm, n_stage=n_stage,
                          n_prompt_tiles=n_prompt_tiles, dec_seq=dec_seq),
        grid=(n_stage + n_tiles,),
        in_specs=in_specs,
        out_specs=[pl.BlockSpec((tm, d), out_tile), pl.BlockSpec((tm, d), out_tile),
                   pl.BlockSpec((tm, d), sample_tile)],
        out_shape=[jax.ShapeDtypeStruct((n_p + n_s, d), F32),
                   jax.ShapeDtypeStruct((n_p + n_s, d), BF16),
                   jax.ShapeDtypeStruct((n_s, d), F32)],
        scratch_shapes=[pltpu.VMEM((d, d), BF16), pltpu.VMEM((tm, d), BF16),
                        pltpu.VMEM((tm, d), BF16)],
        compiler_params=_params(("arbitrary",), 56),
        name="gmlp_gate",
    )(z, z, xp, xs, w_out, ln_g, ln_b, wmix, bias, mix_g, mix_b)


def _ffn_in_kernel(x_ref, wg_ref, wu_ref, o_ref, wg_b, wu_b, *, parts):
    @pl.when(pl.program_id(1) == 0)
    def _():
        wg_b[...] = wg_ref[...].astype(BF16)
        wu_b[...] = wu_ref[...].astype(BF16)

    rows = x_ref.shape[0] // parts
    for p in range(parts):
        rs = slice(p * rows, (p + 1) * rows)
        xb = x_ref[rs, :]
        gate = jnp.dot(xb, wg_b[...], preferred_element_type=F32)
        up = jnp.dot(xb, wu_b[...], preferred_element_type=F32)
        o_ref[rs, :] = (jax.nn.silu(gate) * up).astype(BF16)


def _ffn_in(xb, w_in, layer, tm=2304, th=512, parts=2):
    m, d = xb.shape
    hidden = w_in.shape[2] // 2
    nh = hidden // th
    return pl.pallas_call(
        functools.partial(_ffn_in_kernel, parts=parts),
        grid=(nh, m // tm),
        in_specs=[pl.BlockSpec((tm, d), lambda j, i: (i, 0)),
                  pl.BlockSpec((None, d, th), lambda j, i: (layer, 0, j)),
                  pl.BlockSpec((None, d, th), lambda j, i: (layer, 0, j + nh))],
        out_specs=pl.BlockSpec((tm, th), lambda j, i: (i, j)),
        out_shape=jax.ShapeDtypeStruct((m, hidden), BF16),
        scratch_shapes=[pltpu.VMEM((d, th), BF16), pltpu.VMEM((d, th), BF16)],
        compiler_params=_params(("arbitrary", "arbitrary"), 60),
        name="ffn_in",
    )(xb, w_in, w_in)


def _proj_ln_kernel(*refs, n_stage, n_tiles, src_split, out_split, ln_row):
    n_a = 1 if src_split is None else 2
    a_refs, (w_ref, x_ref, g_ref, b_ref), rest = refs[:n_a], refs[n_a:n_a + 4], refs[n_a + 4:]
    outs, (w_b, h0, h1) = rest[:-3], rest[-3:]
    step = pl.program_id(0)
    t = step - n_stage + 1

    rows = w_ref.shape[0]
    for c in range(n_stage):
        @pl.when(step == c)
        def _(c=c):
            ks = slice(c * rows, (c + 1) * rows)
            w_c = w_ref[...].astype(BF16)
            w_b[ks, :] = w_c
            part = jnp.dot(a_refs[0][:, ks], w_c, preferred_element_type=F32)
            if c == 0:
                h0[...] = part
            else:
                h0[...] += part

    def matmul(dst, second_src):
        a_ref = a_refs[1 if second_src else 0]
        dst[...] = jnp.dot(a_ref[...], w_b[...], preferred_element_type=F32)

    def epilogue(src, second_out):
        ln = slice(ln_row, ln_row + 1)
        x = _layer_norm(ALPHA * x_ref[...] + src[...], g_ref[ln, :], b_ref[ln, :])
        if out_split is None:
            outs[0][...] = x
            outs[1][...] = x.astype(BF16)
        else:
            outs[1 if second_out else 0][...] = x

    for parity, (dst, src) in enumerate(((h0, h1), (h1, h0))):
        for second_src in ((False, True) if src_split is not None else (False,)):
            for second_out in ((False, True) if out_split is not None else (False,)):
                cond = ((t & 1) == parity) & (t >= 1) & (t < n_tiles)
                if src_split is not None:
                    cond &= (t >= src_split) if second_src else (t < src_split)
                if out_split is not None:
                    cond &= (t - 1 >= out_split) if second_out else (t - 1 < out_split)

                @pl.when(cond)
                def _(dst=dst, src=src, second_src=second_src, second_out=second_out):
                    matmul(dst, second_src)
                    epilogue(src, second_out)

    @pl.when(t == n_tiles)
    def _():
        epilogue(h0 if (n_tiles - 1) % 2 == 0 else h1, out_split is not None)


def _proj_ln(a, w, layer, x, ln_g, ln_b, ln_row, chunk_rows, split_rows=None, tm=256):
    a_parts = a if isinstance(a, tuple) else (a,)
    kdim = a_parts[0].shape[1]
    m = sum(p.shape[0] for p in a_parts)
    d = w.shape[2]
    n_stage = kdim // chunk_rows
    n_tiles = m // tm
    clip = lambda v, n: jnp.clip(v, 0, n - 1)
    ln_tile = lambda s: (clip(s - n_stage, n_tiles), 0)
    if len(a_parts) == 1:
        src_split = None
        a_specs = [pl.BlockSpec((tm, kdim), lambda s: (clip(s - n_stage + 1, n_tiles), 0))]
    else:
        src_split = a_parts[0].shape[0] // tm
        assert 1 <= src_split < n_tiles and a_parts[0].shape[0] % tm == 0
        a_specs = [pl.BlockSpec((tm, kdim), lambda s: (clip(s - n_stage + 1, src_split), 0)),
                   pl.BlockSpec((tm, kdim), lambda s: (
                       clip(s - n_stage + 1 - src_split, n_tiles - src_split), 0))]
    in_specs = a_specs + [_chunk_spec(layer, chunk_rows, d, n_stage),
                          pl.BlockSpec((tm, d), ln_tile),
                          _whole(ln_g), _whole(ln_b)]
    if split_rows is None:
        out_split = None
        out_specs = [pl.BlockSpec((tm, d), ln_tile), pl.BlockSpec((tm, d), ln_tile)]
        out_shape = [jax.ShapeDtypeStruct((m, d), F32), jax.ShapeDtypeStruct((m, d), BF16)]
    else:
        out_split = split_rows // tm
        assert 1 <= out_split < n_tiles and split_rows % tm == 0
        out_specs = [
            pl.BlockSpec((tm, d), lambda s: (clip(s - n_stage, out_split), 0)),
            pl.BlockSpec((tm, d), lambda s: (clip(s - n_stage - out_split, n_tiles - out_split), 0))]
        out_shape = [jax.ShapeDtypeStruct((split_rows, d), F32),
                     jax.ShapeDtypeStruct((m - split_rows, d), F32)]
    return pl.pallas_call(
        functools.partial(_proj_ln_kernel, n_stage=n_stage, n_tiles=n_tiles,
                          src_split=src_split, out_split=out_split, ln_row=ln_row),
        grid=(n_stage + n_tiles,),
        in_specs=in_specs,
        out_specs=out_specs,
        out_shape=out_shape,
        scratch_shapes=[pltpu.VMEM((kdim, d), BF16), pltpu.VMEM((tm, d), F32),
                        pltpu.VMEM((tm, d), F32)],
        compiler_params=_params(("arbitrary",), 60),
        name="proj_ln_split" if split_rows else "proj_ln",
    )(*a_parts, w, x, ln_g, ln_b)


def _rope_table_kernel(cos_ref, sin_ref, *, rows, seq, dec_seq):
    r = lax.broadcasted_iota(jnp.int32, (rows, ROPE_HALF), 0) + pl.program_id(0) * rows
    j = lax.broadcasted_iota(jnp.int32, (rows, ROPE_HALF), 1)
    pos = jnp.where(r < seq, r, PAST_LEN + ((r - seq) & (dec_seq - 1)))
    inv = jnp.power(np.float32(ROPE_BASE), -j.astype(F32) / ROPE_HALF)
    ang = pos.astype(F32) * inv
    cos_ref[...] = jnp.cos(ang)
    sin_ref[...] = jnp.sin(ang)


def _rope_tables(seq, dec_seq, rows):
    assert seq % rows == 0 and rows % dec_seq == 0 and dec_seq & (dec_seq - 1) == 0
    spec = pl.BlockSpec((rows, ROPE_HALF), lambda i: (i, 0))
    return pl.pallas_call(
        functools.partial(_rope_table_kernel, rows=rows, seq=seq, dec_seq=dec_seq),
        grid=(seq // rows + 1,),
        out_specs=[spec, spec],
        out_shape=[jax.ShapeDtypeStruct((seq + rows, ROPE_HALF), F32)] * 2,
        compiler_params=_params(("parallel",), 16),
        name="rope_tables",
    )()


def _ret_in_kernel(x_ref, w_ref, *rest, tn, rope, parts):
    if rope:
        cos_ref, sin_ref, o_ref, w_b = rest
    else:
        o_ref, w_b = rest

    @pl.when(pl.program_id(1) == 0)
    def _():
        w_b[...] = w_ref[...].astype(BF16)

    rows = x_ref.shape[0] // parts
    heads, _, width = o_ref.shape
    for p in range(parts):
        rs = slice(p * rows, (p + 1) * rows)
        acc = jnp.dot(x_ref[rs, :], w_b[...], preferred_element_type=F32)
        if not rope:
            for hh in range(heads):
                o_ref[hh, rs, :] = acc[:, hh * width:(hh + 1) * width].astype(BF16)
            continue
        scale = jnp.where(pl.program_id(0) < RET_QK // tn, np.float32(RET_DK ** -0.5),
                          np.float32(1.0))
        cos = cos_ref[rs, :]
        sin = sin_ref[rs, :]
        for hh in range(heads):
            a = acc[:, hh * RET_DK:hh * RET_DK + ROPE_HALF]
            b = acc[:, hh * RET_DK + ROPE_HALF:(hh + 1) * RET_DK]
            o_ref[hh, rs, :ROPE_HALF] = ((a * cos - b * sin) * scale).astype(BF16)
            o_ref[hh, rs, ROPE_HALF:] = ((a * sin + b * cos) * scale).astype(BF16)


def _ret_in(xb, w, col0, n_cols, width, tm, parts, rope=None, tn=1024):
    m, d = xb.shape
    assert tn % width == 0 and (not rope or width == RET_DK)
    in_specs = [pl.BlockSpec((tm, d), lambda j, i: (i, 0)),
                pl.BlockSpec((None, d, tn), lambda j, i: (0, 0, col0 // tn + j))]
    args = [xb, w]
    if rope:
        cos, sin, n_prompt_rows, seq = rope
        assert cos.shape[0] == seq + tm and n_prompt_rows % tm == 0
        per_seq = seq // tm
        table_tile = lambda j, i: (jnp.where(i < n_prompt_rows // tm, i % per_seq, per_seq), 0)
        in_specs += [pl.BlockSpec((tm, ROPE_HALF), table_tile)] * 2
        args += [cos, sin]
    return pl.pallas_call(
        functools.partial(_ret_in_kernel, tn=tn, rope=bool(rope), parts=parts),
        grid=(n_cols // tn, m // tm),
        in_specs=in_specs,
        out_specs=pl.BlockSpec((tn // width, tm, width), lambda j, i: (j, i, 0)),
        out_shape=jax.ShapeDtypeStruct((n_cols // width, m, width), BF16),
        scratch_shapes=[pltpu.VMEM((d, tn), BF16)],
        compiler_params=_params(("arbitrary", "arbitrary"), 60),
        name="ret_in_qk" if rope else "ret_in_vg",
    )(*args)


def _decay_tables(head, blk):
    def log_gamma(shape):
        hv = jnp.full(shape, head, dtype=jnp.int32).astype(F32)
        return jnp.log1p(-jnp.exp2(-5.0 - hv))

    row = lax.broadcasted_iota(jnp.int32, (blk, blk), 0).astype(F32)
    col = lax.broadcasted_iota(jnp.int32, (blk, blk), 1).astype(F32)
    diff = row - col
    decay = jnp.where(diff >= 0, jnp.exp(log_gamma((blk, blk)) * jnp.maximum(diff, 0.0)), 0.0)
    idx_v = lax.broadcasted_iota(jnp.int32, (blk, RET_DV), 0).astype(F32)
    idx_k = lax.broadcasted_iota(jnp.int32, (blk, RET_DK), 0).astype(F32)
    inner = jnp.exp(log_gamma((blk, RET_DV)) * (idx_v + 1.0))
    zeta = jnp.exp(log_gamma((blk, RET_DK)) * (blk - 1.0 - idx_k))
    blk_decay = jnp.exp(log_gamma((1, RET_DV)) * blk)
    return decay, inner, zeta, blk_decay


def _ret_block(qb, kb, vb, gb, state, tables):
    decay, inner, zeta, blk_decay = tables
    scores = lax.dot_general(qb, kb, (((1,), (1,)), ((), ())),
                             preferred_element_type=F32) * decay
    intra = jnp.dot(scores.astype(BF16), vb, preferred_element_type=F32)
    cross = jnp.dot(qb, state.astype(BF16), preferred_element_type=F32) * inner
    kz = (kb.astype(F32) * zeta).T.astype(BF16)
    new_state = blk_decay * state + jnp.dot(kz, vb, preferred_element_type=F32)
    o = intra + cross
    mu = jnp.mean(o, axis=-1, keepdims=True)
    oc = o - mu
    var = jnp.mean(oc * oc, axis=-1, keepdims=True)
    on = oc * lax.rsqrt(var + GN_EPS)
    y = (jax.nn.silu(gb.astype(F32)) * on).astype(BF16)
    return y, new_state


def _ret_kernel(q_ref, k_ref, v_ref, g_ref, qs_ref, ks_ref, vs_ref, gs_ref, s0_ref,
                y_ref, ys_ref, sp_ref, ss_ref, *, blk, n_blk, dec_seq, streams):
    tables = _decay_tables(pl.program_id(0) % RET_HEADS, blk)
    state = jnp.zeros((RET_DK, RET_DV), F32)
    for t in range(n_blk):
        rows = slice(t * blk, (t + 1) * blk)
        y, state = _ret_block(q_ref[rows, :], k_ref[rows, :], v_ref[rows, :],
                              g_ref[rows, :], state, tables)
        y_ref[rows, :] = y
    sp_ref[0, 0] = state
    for i in range(streams):
        rows = slice(i * dec_seq, (i + 1) * dec_seq)
        for h in range(RET_HEADS):
            cv = slice(h * RET_DV, (h + 1) * RET_DV)
            y, new_state = _ret_block(qs_ref[h, rows, :], ks_ref[h, rows, :], vs_ref[h, rows, :],
                                      gs_ref[h, rows, :], s0_ref[0, i, h], _decay_tables(h, dec_seq))
            ys_ref[rows, cv] = y
            ss_ref[i, h] = new_state


def _retention(qk, vg, state0, batch, seq, dec_batch, dec_seq):
    blk = min(seq, RET_CHUNK)
    steps = batch * RET_HEADS
    assert dec_seq <= RET_BLOCK and dec_batch % steps == 0 and (batch * seq) % dec_seq == 0
    streams = dec_batch // steps
    srows = streams * dec_seq
    r_off = batch * seq // srows
    bh = lambda i: (i // RET_HEADS, i % RET_HEADS)
    state_shape = (RET_HEADS, RET_DK, RET_DV)
    return pl.pallas_call(
        functools.partial(_ret_kernel, blk=blk, n_blk=seq // blk, dec_seq=dec_seq,
                          streams=streams),
        grid=(steps,),
        in_specs=[pl.BlockSpec((None, seq, RET_DK), lambda i: (i % RET_HEADS, i // RET_HEADS, 0)),
                  pl.BlockSpec((None, seq, RET_DK),
                               lambda i: (RET_HEADS + i % RET_HEADS, i // RET_HEADS, 0)),
                  pl.BlockSpec((None, seq, RET_DV), lambda i: (i % RET_HEADS, i // RET_HEADS, 0)),
                  pl.BlockSpec((None, seq, RET_DV),
                               lambda i: (RET_HEADS + i % RET_HEADS, i // RET_HEADS, 0)),
                  pl.BlockSpec((RET_HEADS, srows, RET_DK), lambda i: (0, r_off + i, 0)),
                  pl.BlockSpec((RET_HEADS, srows, RET_DK), lambda i: (1, r_off + i, 0)),
                  pl.BlockSpec((RET_HEADS, srows, RET_DV), lambda i: (0, r_off + i, 0)),
                  pl.BlockSpec((RET_HEADS, srows, RET_DV), lambda i: (1, r_off + i, 0)),
                  pl.BlockSpec((1, streams) + state_shape, lambda i: (0, i, 0, 0, 0))],
        out_specs=[pl.BlockSpec((seq, RET_DV), lambda i: bh(i)),
                   pl.BlockSpec((srows, RET_V), lambda i: (i, 0)),
                   pl.BlockSpec((1, 1, RET_DK, RET_DV), lambda i: bh(i) + (0, 0)),
                   pl.BlockSpec((streams,) + state_shape, lambda i: (i, 0, 0, 0))],
        out_shape=[jax.ShapeDtypeStruct((batch * seq, RET_V), BF16),
                   jax.ShapeDtypeStruct((dec_batch * dec_seq, RET_V), BF16),
                   jax.ShapeDtypeStruct((batch, RET_HEADS, RET_DK, RET_DV), F32),
                   jax.ShapeDtypeStruct((dec_batch,) + state_shape, F32)],
        compiler_params=_params(("arbitrary",), 56),
        name="retention",
    )(qk, qk, vg, vg, qk, qk, vg, vg, state0)


def kernel(x_prompt, x_sample, state_ret, w_a_in, a_ln_g, a_ln_b, a_ws, a_bs, w_a_out,
           w_b_in, w_b_out, w_ffn_in, w_ffn_out, ln_mix_g, ln_mix_b, ln_ffn_g, ln_ffn_b):
    batch, seq, d = x_prompt.shape
    dec_batch, dec_seq, _ = x_sample.shape
    n_p = batch * seq
    n_s = dec_batch * dec_seq
    assert d == D_MODEL and seq % GMLP_CHUNK == 0 and GMLP_CHUNK % dec_seq == 0
    assert w_a_in.shape[0] == 1 and w_b_in.shape[0] == 1 and state_ret.shape[0] == 1

    reps = GMLP_CHUNK // dec_seq
    expand = lambda b: jnp.repeat(b.T, GMLP_GROUP_DIM, axis=1)
    wmix = jnp.stack([a_ws[0], jnp.tile(a_ws[0][:, :dec_seq, :dec_seq], (1, reps, reps))])
    bias = jnp.stack([expand(a_bs[0]), expand(jnp.tile(a_bs[0][:, :dec_seq], (1, reps)))])
    xp, xs = x_prompt.reshape(n_p, d), x_sample.reshape(n_s, d)
    z = _gmlp_in(xp, xs, w_a_in)
    x1, x1b, v_sample = _gmlp_gate(z, xp, xs, w_a_out, a_ln_g, a_ln_b, wmix, bias, ln_mix_g,
                                   ln_mix_b, dec_seq)
    act = _ffn_in(x1b, w_ffn_in, 0)
    x2, x2b = _proj_ln(act, w_ffn_out, 0, x1, ln_ffn_g, ln_ffn_b, 0, chunk_rows=512)

    cos, sin = _rope_tables(seq, dec_seq, rows=1024)
    qk = _ret_in(x2b, w_b_in, 0, 2 * RET_QK, RET_DK, tm=1024, parts=1, rope=(cos, sin, n_p, seq))
    vg = _ret_in(x2b, w_b_in, 2 * RET_QK, 2 * RET_V, RET_DV, tm=2304, parts=2)
    y_p, y_s, state_p, state_s = _retention(qk, vg, state_ret, batch, seq, dec_batch, dec_seq)
    x3, x3b = _proj_ln((y_p, y_s), w_b_out, 0, x2, ln_mix_g, ln_mix_b, 1, chunk_rows=512)
    act = _ffn_in(x3b, w_ffn_in, 1)
    y_prompt, y_sample = _proj_ln(act, w_ffn_out, 1, x3, ln_ffn_g, ln_ffn_b, 1, chunk_rows=512,
                                  split_rows=n_p)

    return (y_prompt.reshape(batch, seq, d), y_sample.reshape(dec_batch, dec_seq, d),
            state_p[None], state_s[None], v_sample.reshape(1, dec_batch, dec_seq, d))
```

```python
import functools
import math

import jax
import jax.numpy as jnp
import numpy as np
from jax import lax
from jax.experimental import pallas as pl
from jax.experimental.pallas import tpu as pltpu

D_MODEL = 2048
DEPTH = 2
PAST_LEN = 4096
RET_BLOCK = 64
RET_CHUNK = 256
GMLP_CHUNK = 128
GMLP_GROUPS = 8
GMLP_GROUP_DIM = D_MODEL // GMLP_GROUPS
RET_HEADS = 8
RET_DK = D_MODEL // RET_HEADS
RET_DV = 2 * RET_DK
RET_QK = RET_HEADS * RET_DK
RET_V = RET_HEADS * RET_DV
ROPE_BASE = 10000.0
ROPE_HALF = RET_DK // 2
ALPHA = (2 * DEPTH) ** 0.25
LN_EPS = 1e-5
GN_EPS = 1e-6

F32 = jnp.float32
BF16 = jnp.bfloat16
MIB = 1024 * 1024


def _params(semantics, vmem_mib):
    return pltpu.CompilerParams(dimension_semantics=semantics,
                                vmem_limit_bytes=vmem_mib * MIB)


def _layer_norm(x, g, b):
    mu = jnp.mean(x, axis=-1, keepdims=True)
    xc = x - mu
    var = jnp.mean(xc * xc, axis=-1, keepdims=True)
    return xc * lax.rsqrt(var + LN_EPS) * g + b


def _gelu(x):
    return 0.5 * x * (1.0 + lax.erf(x * np.float32(math.sqrt(0.5))))


def _stage_weight(step, w_ref, wb_ref, n_chunks):
    rows = w_ref.shape[0]

    @pl.when(step < n_chunks)
    def _():
        r0 = pl.multiple_of(step * rows, rows)
        wb_ref[pl.ds(r0, rows), :] = w_ref[...].astype(BF16)


def _whole(a):
    return pl.BlockSpec(a.shape, lambda s: (0,) * a.ndim)


def _chunk_spec(layer, rows, cols, n_chunks):
    return pl.BlockSpec((None, rows, cols),
                        lambda s: (layer, jnp.minimum(s, n_chunks - 1), 0))


def _gmlp_kernel(xp_ref, xs_ref, win_ref, wout_ref, lng_ref, lnb_ref, wmix_ref, bias_ref, g_ref,
                 b_ref, x1_ref, x1b_ref, vn_ref, win_b, wout_b, u_ref, v_ref, s0, s1, xr0, xr1,
                 *, tm, n_stage, n_tiles, n_prompt_tiles, dec_seq):
    step = pl.program_id(0)
    _stage_weight(step, win_ref, win_b, n_stage)
    _stage_weight(step, wout_ref, wout_b, n_stage)
    t = step - n_stage
    d = D_MODEL

    def gate_in(xr_ref, sample):
        xf = (xs_ref if sample else xp_ref)[...]
        xr_ref[...] = xf
        xb = xf.astype(BF16)
        u_ref[...] = _gelu(jnp.dot(xb, win_b[:, :d], preferred_element_type=F32))
        v = _layer_norm(_gelu(jnp.dot(xb, win_b[:, d:], preferred_element_type=F32)),
                        lng_ref[...], lnb_ref[...])
        if sample:
            vn_ref[...] = v
        v_ref[...] = v.astype(BF16)

    def gate_mix(s_ref, sample):
        row = lax.broadcasted_iota(jnp.int32, (GMLP_CHUNK, GMLP_CHUNK), 0)
        col = lax.broadcasted_iota(jnp.int32, (GMLP_CHUNK, GMLP_CHUNK), 1)
        mask = col <= row
        if sample:
            mask = mask & (((row ^ col) & ~(dec_seq - 1)) == 0)
        which = 1 if sample else 0
        for g in range(GMLP_GROUPS):
            w_g = jnp.where(mask, wmix_ref[which, g], 0.0).astype(BF16)
            cs = slice(g * GMLP_GROUP_DIM, (g + 1) * GMLP_GROUP_DIM)
            for seg in range(tm // GMLP_CHUNK):
                rs = slice(seg * GMLP_CHUNK, (seg + 1) * GMLP_CHUNK)
                mixed = jnp.dot(w_g, v_ref[rs, cs], preferred_element_type=F32)
                mixed = mixed + bias_ref[which, :, cs]
                s_ref[rs, cs] = (u_ref[rs, cs] * mixed).astype(BF16)

    def project(s_ref):
        return jnp.dot(s_ref[...], wout_b[...], preferred_element_type=F32)

    def finish(h, xr_ref):
        x1 = _layer_norm(ALPHA * xr_ref[...] + h, g_ref[0:1, :], b_ref[0:1, :])
        x1_ref[...] = x1
        x1b_ref[...] = x1.astype(BF16)

    @pl.when(t == 0)
    def _():
        gate_in(xr0, False)
        gate_mix(s0, False)

    for parity, (cur, prev) in enumerate((((s0, xr0), (s1, xr1)), ((s1, xr1), (s0, xr0)))):
        for sample in (False, True):
            cond = ((t & 1) == parity) & (t >= 1) & (t < n_tiles)
            cond &= (t >= n_prompt_tiles) if sample else (t < n_prompt_tiles)

            @pl.when(cond)
            def _(cur=cur, prev=prev, sample=sample):
                gate_in(cur[1], sample)
                h = project(prev[0])
                gate_mix(cur[0], sample)
                finish(h, prev[1])

    @pl.when(t == n_tiles)
    def _():
        last = (s0, xr0) if (n_tiles - 1) % 2 == 0 else (s1, xr1)
        finish(project(last[0]), last[1])


def _gmlp_layer(xp, xs, w_in, w_out, ln_g, ln_b, wmix, bias, mix_g, mix_b, dec_seq,
                tm=256, n_stage=32):
    (n_p, d), n_s = xp.shape, xs.shape[0]
    n_prompt_tiles, n_sample_tiles = n_p // tm, n_s // tm
    n_tiles = n_prompt_tiles + n_sample_tiles
    assert n_p % tm == 0 and n_s % tm == 0 and n_sample_tiles >= 1
    clip = lambda v, n: jnp.clip(v, 0, n - 1)
    sample_tile = lambda s: (clip(s - n_stage - n_prompt_tiles, n_sample_tiles), 0)
    out_tile = lambda s: (clip(s - n_stage - 1, n_tiles), 0)
    once = pl.Buffered(1)
    in_specs = [pl.BlockSpec((tm, d), lambda s: (clip(s - n_stage, n_prompt_tiles), 0)),
                pl.BlockSpec((tm, d), sample_tile),
                _chunk_spec(0, d // n_stage, 2 * d, n_stage),
                _chunk_spec(0, d // n_stage, d, n_stage),
                _whole(ln_g), _whole(ln_b),
                pl.BlockSpec((2, GMLP_GROUPS, GMLP_CHUNK, GMLP_CHUNK), lambda s: (0, 0, 0, 0),
                             pipeline_mode=once),
                pl.BlockSpec((2, GMLP_CHUNK, d), lambda s: (0, 0, 0), pipeline_mode=once),
                _whole(mix_g), _whole(mix_b)]
    return pl.pallas_call(
        functools.partial(_gmlp_kernel, tm=tm, n_stage=n_stage, n_tiles=n_tiles,
                          n_prompt_tiles=n_prompt_tiles, dec_seq=dec_seq),
        grid=(n_stage + n_tiles + 1,),
        in_specs=in_specs,
        out_specs=[pl.BlockSpec((tm, d), out_tile), pl.BlockSpec((tm, d), out_tile),
                   pl.BlockSpec((tm, d), sample_tile)],
        out_shape=[jax.ShapeDtypeStruct((n_p + n_s, d), F32),
                   jax.ShapeDtypeStruct((n_p + n_s, d), BF16),
                   jax.ShapeDtypeStruct((n_s, d), F32)],
        scratch_shapes=[pltpu.VMEM((d, 2 * d), BF16), pltpu.VMEM((d, d), BF16),
                        pltpu.VMEM((tm, d), F32), pltpu.VMEM((tm, d), BF16),
                        pltpu.VMEM((tm, d), BF16), pltpu.VMEM((tm, d), BF16),
                        pltpu.VMEM((tm, d), F32), pltpu.VMEM((tm, d), F32)],
        compiler_params=_params(("arbitrary",), 62),
        name="gmlp",
    )(xp, xs, w_in, w_out, ln_g, ln_b, wmix, bias, mix_g, mix_b)


def _ffn_in_kernel(x_ref, wg_ref, wu_ref, o_ref, wg_b, wu_b, *, parts):
    @pl.when(pl.program_id(1) == 0)
    def _():
        wg_b[...] = wg_ref[...].astype(BF16)
        wu_b[...] = wu_ref[...].astype(BF16)

    rows = x_ref.shape[0] // parts
    for p in range(parts):
        rs = slice(p * rows, (p + 1) * rows)
        xb = x_ref[rs, :]
        gate = jnp.dot(xb, wg_b[...], preferred_element_type=F32)
        up = jnp.dot(xb, wu_b[...], preferred_element_type=F32)
        o_ref[rs, :] = (jax.nn.silu(gate) * up).astype(BF16)


def _ffn_in(xb, w_in, layer, tm=3072, th=512, parts=4):
    m, d = xb.shape
    hidden = w_in.shape[2] // 2
    nh = hidden // th
    return pl.pallas_call(
        functools.partial(_ffn_in_kernel, parts=parts),
        grid=(nh, m // tm),
        in_specs=[pl.BlockSpec((tm, d), lambda j, i: (i, 0)),
                  pl.BlockSpec((None, d, th), lambda j, i: (layer, 0, j)),
                  pl.BlockSpec((None, d, th), lambda j, i: (layer, 0, j + nh))],
        out_specs=pl.BlockSpec((tm, th), lambda j, i: (i, j)),
        out_shape=jax.ShapeDtypeStruct((m, hidden), BF16),
        scratch_shapes=[pltpu.VMEM((d, th), BF16), pltpu.VMEM((d, th), BF16)],
        compiler_params=_params(("arbitrary", "arbitrary"), 60),
        name="ffn_in",
    )(xb, w_in, w_in)


def _proj_ln_kernel(*refs, n_stage, n_tiles, src_split, out_split, ln_row):
    n_a = 1 if src_split is None else 2
    a_refs, (w_ref, x_ref, g_ref, b_ref), rest = refs[:n_a], refs[n_a:n_a + 4], refs[n_a + 4:]
    outs, (w_b, h0, h1) = rest[:-3], rest[-3:]
    step = pl.program_id(0)
    t = step - n_stage + 1

    rows = w_ref.shape[0]
    for c in range(n_stage):
        @pl.when(step == c)
        def _(c=c):
            ks = slice(c * rows, (c + 1) * rows)
            w_c = w_ref[...].astype(BF16)
            w_b[ks, :] = w_c
            part = jnp.dot(a_refs[0][:, ks], w_c, preferred_element_type=F32)
            if c == 0:
                h0[...] = part
            else:
                h0[...] += part

    def matmul(dst, second_src):
        a_ref = a_refs[1 if second_src else 0]
        dst[...] = jnp.dot(a_ref[...], w_b[...], preferred_element_type=F32)

    def epilogue(src, second_out):
        ln = slice(ln_row, ln_row + 1)
        x = _layer_norm(ALPHA * x_ref[...] + src[...], g_ref[ln, :], b_ref[ln, :])
        if out_split is None:
            outs[0][...] = x
            outs[1][...] = x.astype(BF16)
        else:
            outs[1 if second_out else 0][...] = x

    for parity, (dst, src) in enumerate(((h0, h1), (h1, h0))):
        for second_src in ((False, True) if src_split is not None else (False,)):
            for second_out in ((False, True) if out_split is not None else (False,)):
                cond = ((t & 1) == parity) & (t >= 1) & (t < n_tiles)
                if src_split is not None:
                    cond &= (t >= src_split) if second_src else (t < src_split)
                if out_split is not None:
                    cond &= (t - 1 >= out_split) if second_out else (t - 1 < out_split)

                @pl.when(cond)
                def _(dst=dst, src=src, second_src=second_src, second_out=second_out):
                    matmul(dst, second_src)
                    epilogue(src, second_out)

    @pl.when(t == n_tiles)
    def _():
        epilogue(h0 if (n_tiles - 1) % 2 == 0 else h1, out_split is not None)


def _proj_ln(a, w, layer, x, ln_g, ln_b, ln_row, chunk_rows, split_rows=None, tm=256):
    a_parts = a if isinstance(a, tuple) else (a,)
    kdim = a_parts[0].shape[1]
    m = sum(p.shape[0] for p in a_parts)
    d = w.shape[2]
    n_stage = kdim // chunk_rows
    n_tiles = m // tm
    clip = lambda v, n: jnp.clip(v, 0, n - 1)
    ln_tile = lambda s: (clip(s - n_stage, n_tiles), 0)
    if len(a_parts) == 1:
        src_split = None
        a_specs = [pl.BlockSpec((tm, kdim), lambda s: (clip(s - n_stage + 1, n_tiles), 0))]
    else:
        src_split = a_parts[0].shape[0] // tm
        assert 1 <= src_split < n_tiles and a_parts[0].shape[0] % tm == 0
        a_specs = [pl.BlockSpec((tm, kdim), lambda s: (clip(s - n_stage + 1, src_split), 0)),
                   pl.BlockSpec((tm, kdim), lambda s: (
                       clip(s - n_stage + 1 - src_split, n_tiles - src_split), 0))]
    in_specs = a_specs + [_chunk_spec(layer, chunk_rows, d, n_stage),
                          pl.BlockSpec((tm, d), ln_tile),
                          _whole(ln_g), _whole(ln_b)]
    if split_rows is None:
        out_split = None
        out_specs = [pl.BlockSpec((tm, d), ln_tile), pl.BlockSpec((tm, d), ln_tile)]
        out_shape = [jax.ShapeDtypeStruct((m, d), F32), jax.ShapeDtypeStruct((m, d), BF16)]
    else:
        out_split = split_rows // tm
        assert 1 <= out_split < n_tiles and split_rows % tm == 0
        out_specs = [
            pl.BlockSpec((tm, d), lambda s: (clip(s - n_stage, out_split), 0)),
            pl.BlockSpec((tm, d), lambda s: (clip(s - n_stage - out_split, n_tiles - out_split), 0))]
        out_shape = [jax.ShapeDtypeStruct((split_rows, d), F32),
                     jax.ShapeDtypeStruct((m - split_rows, d), F32)]
    return pl.pallas_call(
        functools.partial(_proj_ln_kernel, n_stage=n_stage, n_tiles=n_tiles,
                          src_split=src_split, out_split=out_split, ln_row=ln_row),
        grid=(n_stage + n_tiles,),
        in_specs=in_specs,
        out_specs=out_specs,
        out_shape=out_shape,
        scratch_shapes=[pltpu.VMEM((kdim, d), BF16), pltpu.VMEM((tm, d), F32),
                        pltpu.VMEM((tm, d), F32)],
        compiler_params=_params(("arbitrary",), 60),
        name="proj_ln_split" if split_rows else "proj_ln",
    )(*a_parts, w, x, ln_g, ln_b)


def _rope_table_kernel(cos_ref, sin_ref, *, rows, seq, dec_seq):
    r = lax.broadcasted_iota(jnp.int32, (rows, ROPE_HALF), 0) + pl.program_id(0) * rows
    j = lax.broadcasted_iota(jnp.int32, (rows, ROPE_HALF), 1)
    pos = jnp.where(r < seq, r, PAST_LEN + ((r - seq) & (dec_seq - 1)))
    inv = jnp.power(np.float32(ROPE_BASE), -j.astype(F32) / ROPE_HALF)
    ang = pos.astype(F32) * inv
    cos_ref[...] = jnp.cos(ang)
    sin_ref[...] = jnp.sin(ang)


def _rope_tables(seq, dec_seq, rows):
    assert seq % rows == 0 and rows % dec_seq == 0 and dec_seq & (dec_seq - 1) == 0
    spec = pl.BlockSpec((rows, ROPE_HALF), lambda i: (i, 0))
    return pl.pallas_call(
        functools.partial(_rope_table_kernel, rows=rows, seq=seq, dec_seq=dec_seq),
        grid=(seq // rows + 1,),
        out_specs=[spec, spec],
        out_shape=[jax.ShapeDtypeStruct((seq + rows, ROPE_HALF), F32)] * 2,
        compiler_params=_params(("parallel",), 16),
        name="rope_tables",
    )()


def _ret_in_kernel(x_ref, w_ref, *rest, tn, rope, parts):
    if rope:
        cos_ref, sin_ref, o_ref, w_b = rest
    else:
        o_ref, w_b = rest

    @pl.when(pl.program_id(1) == 0)
    def _():
        w_b[...] = w_ref[...].astype(BF16)

    rows = x_ref.shape[0] // parts
    heads, _, width = o_ref.shape
    for p in range(parts):
        rs = slice(p * rows, (p + 1) * rows)
        acc = jnp.dot(x_ref[rs, :], w_b[...], preferred_element_type=F32)
        if not rope:
            for hh in range(heads):
                o_ref[hh, rs, :] = acc[:, hh * width:(hh + 1) * width].astype(BF16)
            continue
        scale = jnp.where(pl.program_id(0) < RET_QK // tn, np.float32(RET_DK ** -0.5),
                          np.float32(1.0))
        cos = cos_ref[rs, :]
        sin = sin_ref[rs, :]
        for hh in range(heads):
            a = acc[:, hh * RET_DK:hh * RET_DK + ROPE_HALF]
            b = acc[:, hh * RET_DK + ROPE_HALF:(hh + 1) * RET_DK]
            o_ref[hh, rs, :ROPE_HALF] = ((a * cos - b * sin) * scale).astype(BF16)
            o_ref[hh, rs, ROPE_HALF:] = ((a * sin + b * cos) * scale).astype(BF16)


def _ret_in(xb, w, col0, n_cols, width, tm, parts, rope=None, tn=1024):
    m, d = xb.shape
    assert tn % width == 0 and (not rope or width == RET_DK)
    in_specs = [pl.BlockSpec((tm, d), lambda j, i: (i, 0)),
                pl.BlockSpec((None, d, tn), lambda j, i: (0, 0, col0 // tn + j))]
    args = [xb, w]
    if rope:
        cos, sin, n_prompt_rows, seq = rope
        assert cos.shape[0] == seq + tm and n_prompt_rows % tm == 0
        per_seq = seq // tm
        table_tile = lambda j, i: (jnp.where(i < n_prompt_rows // tm, i % per_seq, per_seq), 0)
        in_specs += [pl.BlockSpec((tm, ROPE_HALF), table_tile)] * 2
        args += [cos, sin]
    return pl.pallas_call(
        functools.partial(_ret_in_kernel, tn=tn, rope=bool(rope), parts=parts),
        grid=(n_cols // tn, m // tm),
        in_specs=in_specs,
        out_specs=pl.BlockSpec((tn // width, tm, width), lambda j, i: (j, i, 0)),
        out_shape=jax.ShapeDtypeStruct((n_cols // width, m, width), BF16),
        scratch_shapes=[pltpu.VMEM((d, tn), BF16)],
        compiler_params=_params(("arbitrary", "arbitrary"), 60),
        name="ret_in_qk" if rope else "ret_in_vg",
    )(*args)


def _decay_tables(head, blk):
    def log_gamma(shape):
        hv = jnp.full(shape, head, dtype=jnp.int32).astype(F32)
        return jnp.log1p(-jnp.exp2(-5.0 - hv))

    row = lax.broadcasted_iota(jnp.int32, (blk, blk), 0).astype(F32)
    col = lax.broadcasted_iota(jnp.int32, (blk, blk), 1).astype(F32)
    diff = row - col
    decay = jnp.where(diff >= 0, jnp.exp(log_gamma((blk, blk)) * jnp.maximum(diff, 0.0)), 0.0)
    idx_v = lax.broadcasted_iota(jnp.int32, (blk, RET_DV), 0).astype(F32)
    idx_k = lax.broadcasted_iota(jnp.int32, (blk, RET_DK), 0).astype(F32)
    inner = jnp.exp(log_gamma((blk, RET_DV)) * (idx_v + 1.0))
    zeta = jnp.exp(log_gamma((blk, RET_DK)) * (blk - 1.0 - idx_k))
    blk_decay = jnp.exp(log_gamma((1, RET_DV)) * blk)
    return decay, inner, zeta, blk_decay


def _ret_block(qb, kb, vb, gb, state, tables):
    decay, inner, zeta, blk_decay = tables
    scores = lax.dot_general(qb, kb, (((1,), (1,)), ((), ())),
                             preferred_element_type=F32) * decay
    intra = jnp.dot(scores.astype(BF16), vb, preferred_element_type=F32)
    cross = jnp.dot(qb, state.astype(BF16), preferred_element_type=F32) * inner
    kz = (kb.astype(F32) * zeta).T.astype(BF16)
    new_state = blk_decay * state + jnp.dot(kz, vb, preferred_element_type=F32)
    o = intra + cross
    mu = jnp.mean(o, axis=-1, keepdims=True)
    oc = o - mu
    var = jnp.mean(oc * oc, axis=-1, keepdims=True)
    on = oc * lax.rsqrt(var + GN_EPS)
    y = (jax.nn.silu(gb.astype(F32)) * on).astype(BF16)
    return y, new_state


def _ret_kernel(q_ref, k_ref, v_ref, g_ref, qs_ref, ks_ref, vs_ref, gs_ref, s0_ref,
                y_ref, ys_ref, sp_ref, ss_ref, *, blk, n_blk, dec_seq, streams):
    tables = _decay_tables(pl.program_id(0) % RET_HEADS, blk)
    state = jnp.zeros((RET_DK, RET_DV), F32)
    for t in range(n_blk):
        rows = slice(t * blk, (t + 1) * blk)
        y, state = _ret_block(q_ref[rows, :], k_ref[rows, :], v_ref[rows, :],
                              g_ref[rows, :], state, tables)
        y_ref[rows, :] = y
    sp_ref[0, 0] = state
    for i in range(streams):
        rows = slice(i * dec_seq, (i + 1) * dec_seq)
        for h in range(RET_HEADS):
            cv = slice(h * RET_DV, (h + 1) * RET_DV)
            y, new_state = _ret_block(qs_ref[h, rows, :], ks_ref[h, rows, :], vs_ref[h, rows, :],
                                      gs_ref[h, rows, :], s0_ref[0, i, h], _decay_tables(h, dec_seq))
            ys_ref[rows, cv] = y
            ss_ref[i, h] = new_state


def _retention(qk, vg, state0, batch, seq, dec_batch, dec_seq):
    blk = min(seq, RET_CHUNK)
    steps = batch * RET_HEADS
    assert dec_seq <= RET_BLOCK and dec_batch % steps == 0 and (batch * seq) % dec_seq == 0
    streams = dec_batch // steps
    srows = streams * dec_seq
    r_off = batch * seq // srows
    bh = lambda i: (i // RET_HEADS, i % RET_HEADS)
    state_shape = (RET_HEADS, RET_DK, RET_DV)
    return pl.pallas_call(
        functools.partial(_ret_kernel, blk=blk, n_blk=seq // blk, dec_seq=dec_seq,
                          streams=streams),
        grid=(steps,),
        in_specs=[pl.BlockSpec((None, seq, RET_DK), lambda i: (i % RET_HEADS, i // RET_HEADS, 0)),
                  pl.BlockSpec((None, seq, RET_DK),
                               lambda i: (RET_HEADS + i % RET_HEADS, i // RET_HEADS, 0)),
                  pl.BlockSpec((None, seq, RET_DV), lambda i: (i % RET_HEADS, i // RET_HEADS, 0)),
                  pl.BlockSpec((None, seq, RET_DV),
                               lambda i: (RET_HEADS + i % RET_HEADS, i // RET_HEADS, 0)),
                  pl.BlockSpec((RET_HEADS, srows, RET_DK), lambda i: (0, r_off + i, 0)),
                  pl.BlockSpec((RET_HEADS, srows, RET_DK), lambda i: (1, r_off + i, 0)),
                  pl.BlockSpec((RET_HEADS, srows, RET_DV), lambda i: (0, r_off + i, 0)),
                  pl.BlockSpec((RET_HEADS, srows, RET_DV), lambda i: (1, r_off + i, 0)),
                  pl.BlockSpec((1, streams) + state_shape, lambda i: (0, i, 0, 0, 0))],
        out_specs=[pl.BlockSpec((seq, RET_DV), lambda i: bh(i)),
                   pl.BlockSpec((srows, RET_V), lambda i: (i, 0)),
                   pl.BlockSpec((1, 1, RET_DK, RET_DV), lambda i: bh(i) + (0, 0)),
                   pl.BlockSpec((streams,) + state_shape, lambda i: (i, 0, 0, 0))],
        out_shape=[jax.ShapeDtypeStruct((batch * seq, RET_V), BF16),
                   jax.ShapeDtypeStruct((dec_batch * dec_seq, RET_V), BF16),
                   jax.ShapeDtypeStruct((batch, RET_HEADS, RET_DK, RET_DV), F32),
                   jax.ShapeDtypeStruct((dec_batch,) + state_shape, F32)],
        compiler_params=_params(("arbitrary",), 56),
        name="retention",
    )(qk, qk, vg, vg, qk, qk, vg, vg, state0)


def kernel(x_prompt, x_sample, state_ret, w_a_in, a_ln_g, a_ln_b, a_ws, a_bs, w_a_out,
           w_b_in, w_b_out, w_ffn_in, w_ffn_out, ln_mix_g, ln_mix_b, ln_ffn_g, ln_ffn_b):
    batch, seq, d = x_prompt.shape
    dec_batch, dec_seq, _ = x_sample.shape
    n_p = batch * seq
    n_s = dec_batch * dec_seq
    assert d == D_MODEL and seq % GMLP_CHUNK == 0 and GMLP_CHUNK % dec_seq == 0
    assert w_a_in.shape[0] == 1 and w_b_in.shape[0] == 1 and state_ret.shape[0] == 1

    reps = GMLP_CHUNK // dec_seq
    expand = lambda b: jnp.repeat(b.T, GMLP_GROUP_DIM, axis=1)
    wmix = jnp.stack([a_ws[0], jnp.tile(a_ws[0][:, :dec_seq, :dec_seq], (1, reps, reps))])
    bias = jnp.stack([expand(a_bs[0]), expand(jnp.tile(a_bs[0][:, :dec_seq], (1, reps)))])
    x1, x1b, v_sample = _gmlp_layer(x_prompt.reshape(n_p, d), x_sample.reshape(n_s, d), w_a_in,
                                    w_a_out, a_ln_g, a_ln_b, wmix, bias, ln_mix_g, ln_mix_b, dec_seq)
    act = _ffn_in(x1b, w_ffn_in, 0)
    x2, x2b = _proj_ln(act, w_ffn_out, 0, x1, ln_ffn_g, ln_ffn_b, 0, chunk_rows=512)

    cos, sin = _rope_tables(seq, dec_seq, rows=1024)
    qk = _ret_in(x2b, w_b_in, 0, 2 * RET_QK, RET_DK, tm=1024, parts=1, rope=(cos, sin, n_p, seq))
    vg = _ret_in(x2b, w_b_in, 2 * RET_QK, 2 * RET_V, RET_DV, tm=3072, parts=3)
    y_p, y_s, state_p, state_s = _retention(qk, vg, state_ret, batch, seq, dec_batch, dec_seq)
    x3, x3b = _proj_ln((y_p, y_s), w_b_out, 0, x2, ln_mix_g, ln_mix_b, 1, chunk_rows=512)
    act = _ffn_in(x3b, w_ffn_in, 1)
    y_prompt, y_sample = _proj_ln(act, w_ffn_out, 1, x3, ln_ffn_g, ln_ffn_b, 1, chunk_rows=512,
                                  split_rows=n_p)

    return (y_prompt.reshape(batch, seq, d), y_sample.reshape(dec_batch, dec_seq, d),
            state_p[None], state_s[None], v_sample.reshape(1, dec_batch, dec_seq, d))
```

```python
import functools
import math

import jax
import jax.numpy as jnp
import numpy as np
from jax import lax
from jax.experimental import pallas as pl
from jax.experimental.pallas import tpu as pltpu

D_MODEL = 2048
DEPTH = 2
PAST_LEN = 4096
RET_BLOCK = 64
RET_CHUNK = 256
STATE_RING = 3
GMLP_CHUNK = 128
GMLP_GROUPS = 8
GMLP_GROUP_DIM = D_MODEL // GMLP_GROUPS
RET_HEADS = 8
RET_DK = D_MODEL // RET_HEADS
RET_DV = 2 * RET_DK
RET_QK = RET_HEADS * RET_DK
RET_V = RET_HEADS * RET_DV
ROPE_BASE = 10000.0
ROPE_HALF = RET_DK // 2
ALPHA = (2 * DEPTH) ** 0.25
LN_EPS = 1e-5
GN_EPS = 1e-6

F32 = jnp.float32
BF16 = jnp.bfloat16
MIB = 1024 * 1024


def _params(semantics, vmem_mib):
    return pltpu.CompilerParams(dimension_semantics=semantics,
                                vmem_limit_bytes=vmem_mib * MIB)


def _layer_norm(x, g, b):
    mu = jnp.mean(x, axis=-1, keepdims=True)
    xc = x - mu
    var = jnp.mean(xc * xc, axis=-1, keepdims=True)
    return xc * lax.rsqrt(var + LN_EPS) * g + b


def _gelu(x):
    return 0.5 * x * (1.0 + lax.erf(x * np.float32(math.sqrt(0.5))))


def _stage_weight(step, w_ref, wb_ref, n_chunks):
    rows = w_ref.shape[0]

    @pl.when(step < n_chunks)
    def _():
        r0 = pl.multiple_of(step * rows, rows)
        wb_ref[pl.ds(r0, rows), :] = w_ref[...].astype(BF16)


def _whole(a):
    return pl.BlockSpec(a.shape, lambda s: (0,) * a.ndim)


def _chunk_spec(layer, rows, cols, n_chunks):
    return pl.BlockSpec((None, rows, cols),
                        lambda s: (layer, jnp.minimum(s, n_chunks - 1), 0))


def _gmlp_kernel(xp_ref, xs_ref, win_ref, wout_ref, lng_ref, lnb_ref, wmix_ref, bias_ref, g_ref,
                 b_ref, x1_ref, x1b_ref, vn_ref, win_b, wout_b, u_ref, v_ref, s0, s1, xr0, xr1,
                 *, tm, n_stage, n_tiles, n_prompt_tiles, dec_seq):
    step = pl.program_id(0)
    _stage_weight(step, win_ref, win_b, n_stage)
    _stage_weight(step, wout_ref, wout_b, n_stage)
    t = step - n_stage
    d = D_MODEL

    def gate_in(xr_ref, sample):
        xf = (xs_ref if sample else xp_ref)[...]
        xr_ref[...] = xf
        xb = xf.astype(BF16)
        u_ref[...] = _gelu(jnp.dot(xb, win_b[:, :d], preferred_element_type=F32))
        v = _layer_norm(_gelu(jnp.dot(xb, win_b[:, d:], preferred_element_type=F32)),
                        lng_ref[...], lnb_ref[...])
        if sample:
            vn_ref[...] = v
        v_ref[...] = v.astype(BF16)

    def gate_mix(s_ref, sample):
        row = lax.broadcasted_iota(jnp.int32, (GMLP_CHUNK, GMLP_CHUNK), 0)
        col = lax.broadcasted_iota(jnp.int32, (GMLP_CHUNK, GMLP_CHUNK), 1)
        mask = col <= row
        if sample:
            mask = mask & (((row ^ col) & ~(dec_seq - 1)) == 0)
        which = 1 if sample else 0
        for g in range(GMLP_GROUPS):
            w_g = jnp.where(mask, wmix_ref[which, g], 0.0).astype(BF16)
            cs = slice(g * GMLP_GROUP_DIM, (g + 1) * GMLP_GROUP_DIM)
            for seg in range(tm // GMLP_CHUNK):
                rs = slice(seg * GMLP_CHUNK, (seg + 1) * GMLP_CHUNK)
                mixed = jnp.dot(w_g, v_ref[rs, cs], preferred_element_type=F32)
                mixed = mixed + bias_ref[which, :, cs]
                s_ref[rs, cs] = (u_ref[rs, cs] * mixed).astype(BF16)

    def project(s_ref):
        return jnp.dot(s_ref[...], wout_b[...], preferred_element_type=F32)

    def finish(h, xr_ref):
        x1 = _layer_norm(ALPHA * xr_ref[...] + h, g_ref[0:1, :], b_ref[0:1, :])
        x1_ref[...] = x1
        x1b_ref[...] = x1.astype(BF16)

    @pl.when(t == 0)
    def _():
        gate_in(xr0, False)
        gate_mix(s0, False)

    for parity, (cur, prev) in enumerate((((s0, xr0), (s1, xr1)), ((s1, xr1), (s0, xr0)))):
        for sample in (False, True):
            cond = ((t & 1) == parity) & (t >= 1) & (t < n_tiles)
            cond &= (t >= n_prompt_tiles) if sample else (t < n_prompt_tiles)

            @pl.when(cond)
            def _(cur=cur, prev=prev, sample=sample):
                gate_in(cur[1], sample)
                h = project(prev[0])
                gate_mix(cur[0], sample)
                finish(h, prev[1])

    @pl.when(t == n_tiles)
    def _():
        last = (s0, xr0) if (n_tiles - 1) % 2 == 0 else (s1, xr1)
        finish(project(last[0]), last[1])


def _gmlp_layer(xp, xs, w_in, w_out, ln_g, ln_b, wmix, bias, mix_g, mix_b, dec_seq,
                tm=256, n_stage=32):
    (n_p, d), n_s = xp.shape, xs.shape[0]
    n_prompt_tiles, n_sample_tiles = n_p // tm, n_s // tm
    n_tiles = n_prompt_tiles + n_sample_tiles
    assert n_p % tm == 0 and n_s % tm == 0 and n_sample_tiles >= 1
    clip = lambda v, n: jnp.clip(v, 0, n - 1)
    sample_tile = lambda s: (clip(s - n_stage - n_prompt_tiles, n_sample_tiles), 0)
    out_tile = lambda s: (clip(s - n_stage - 1, n_tiles), 0)
    once = pl.Buffered(1)
    in_specs = [pl.BlockSpec((tm, d), lambda s: (clip(s - n_stage, n_prompt_tiles), 0)),
                pl.BlockSpec((tm, d), sample_tile),
                _chunk_spec(0, d // n_stage, 2 * d, n_stage),
                _chunk_spec(0, d // n_stage, d, n_stage),
                _whole(ln_g), _whole(ln_b),
                pl.BlockSpec((2, GMLP_GROUPS, GMLP_CHUNK, GMLP_CHUNK), lambda s: (0, 0, 0, 0),
                             pipeline_mode=once),
                pl.BlockSpec((2, GMLP_CHUNK, d), lambda s: (0, 0, 0), pipeline_mode=once),
                _whole(mix_g), _whole(mix_b)]
    return pl.pallas_call(
        functools.partial(_gmlp_kernel, tm=tm, n_stage=n_stage, n_tiles=n_tiles,
                          n_prompt_tiles=n_prompt_tiles, dec_seq=dec_seq),
        grid=(n_stage + n_tiles + 1,),
        in_specs=in_specs,
        out_specs=[pl.BlockSpec((tm, d), out_tile), pl.BlockSpec((tm, d), out_tile),
                   pl.BlockSpec((tm, d), sample_tile)],
        out_shape=[jax.ShapeDtypeStruct((n_p + n_s, d), F32),
                   jax.ShapeDtypeStruct((n_p + n_s, d), BF16),
                   jax.ShapeDtypeStruct((n_s, d), F32)],
        scratch_shapes=[pltpu.VMEM((d, 2 * d), BF16), pltpu.VMEM((d, d), BF16),
                        pltpu.VMEM((tm, d), F32), pltpu.VMEM((tm, d), BF16),
                        pltpu.VMEM((tm, d), BF16), pltpu.VMEM((tm, d), BF16),
                        pltpu.VMEM((tm, d), F32), pltpu.VMEM((tm, d), F32)],
        compiler_params=_params(("arbitrary",), 62),
        name="gmlp",
    )(xp, xs, w_in, w_out, ln_g, ln_b, wmix, bias, mix_g, mix_b)


def _ffn_in_kernel(x_ref, wg_ref, wu_ref, o_ref, wg_b, wu_b, *, parts):
    @pl.when(pl.program_id(1) == 0)
    def _():
        wg_b[...] = wg_ref[...].astype(BF16)
        wu_b[...] = wu_ref[...].astype(BF16)

    rows = x_ref.shape[0] // parts
    for p in range(parts):
        rs = slice(p * rows, (p + 1) * rows)
        xb = x_ref[rs, :]
        gate = jnp.dot(xb, wg_b[...], preferred_element_type=F32)
        up = jnp.dot(xb, wu_b[...], preferred_element_type=F32)
        o_ref[rs, :] = (jax.nn.silu(gate) * up).astype(BF16)


def _ffn_in(xb, w_in, layer, tm=3072, th=512, parts=4):
    m, d = xb.shape
    hidden = w_in.shape[2] // 2
    nh = hidden // th
    return pl.pallas_call(
        functools.partial(_ffn_in_kernel, parts=parts),
        grid=(nh, m // tm),
        in_specs=[pl.BlockSpec((tm, d), lambda j, i: (i, 0)),
                  pl.BlockSpec((None, d, th), lambda j, i: (layer, 0, j)),
                  pl.BlockSpec((None, d, th), lambda j, i: (layer, 0, j + nh))],
        out_specs=pl.BlockSpec((tm, th), lambda j, i: (i, j)),
        out_shape=jax.ShapeDtypeStruct((m, hidden), BF16),
        scratch_shapes=[pltpu.VMEM((d, th), BF16), pltpu.VMEM((d, th), BF16)],
        compiler_params=_params(("arbitrary", "arbitrary"), 60),
        name="ffn_in",
    )(xb, w_in, w_in)


def _proj_ln_kernel(*refs, n_stage, n_tiles, src_split, out_split, ln_row):
    n_a = 1 if src_split is None else 2
    a_refs, (w_ref, x_ref, g_ref, b_ref), rest = refs[:n_a], refs[n_a:n_a + 4], refs[n_a + 4:]
    outs, (w_b, h0, h1) = rest[:-3], rest[-3:]
    step = pl.program_id(0)
    t = step - n_stage + 1

    rows = w_ref.shape[0]
    for c in range(n_stage):
        @pl.when(step == c)
        def _(c=c):
            ks = slice(c * rows, (c + 1) * rows)
            w_c = w_ref[...].astype(BF16)
            w_b[ks, :] = w_c
            part = jnp.dot(a_refs[0][:, ks], w_c, preferred_element_type=F32)
            if c == 0:
                h0[...] = part
            else:
                h0[...] += part

    def matmul(dst, second_src):
        a_ref = a_refs[1 if second_src else 0]
        dst[...] = jnp.dot(a_ref[...], w_b[...], preferred_element_type=F32)

    def epilogue(src, second_out):
        ln = slice(ln_row, ln_row + 1)
        x = _layer_norm(ALPHA * x_ref[...] + src[...], g_ref[ln, :], b_ref[ln, :])
        if out_split is None:
            outs[0][...] = x
            outs[1][...] = x.astype(BF16)
        else:
            outs[1 if second_out else 0][...] = x

    for parity, (dst, src) in enumerate(((h0, h1), (h1, h0))):
        for second_src in ((False, True) if src_split is not None else (False,)):
            for second_out in ((False, True) if out_split is not None else (False,)):
                cond = ((t & 1) == parity) & (t >= 1) & (t < n_tiles)
                if src_split is not None:
                    cond &= (t >= src_split) if second_src else (t < src_split)
                if out_split is not None:
                    cond &= (t - 1 >= out_split) if second_out else (t - 1 < out_split)

                @pl.when(cond)
                def _(dst=dst, src=src, second_src=second_src, second_out=second_out):
                    matmul(dst, second_src)
                    epilogue(src, second_out)

    @pl.when(t == n_tiles)
    def _():
        epilogue(h0 if (n_tiles - 1) % 2 == 0 else h1, out_split is not None)


def _proj_ln(a, w, layer, x, ln_g, ln_b, ln_row, chunk_rows, split_rows=None, tm=256):
    a_parts = a if isinstance(a, tuple) else (a,)
    kdim = a_parts[0].shape[1]
    m = sum(p.shape[0] for p in a_parts)
    d = w.shape[2]
    n_stage = kdim // chunk_rows
    n_tiles = m // tm
    clip = lambda v, n: jnp.clip(v, 0, n - 1)
    ln_tile = lambda s: (clip(s - n_stage, n_tiles), 0)
    if len(a_parts) == 1:
        src_split = None
        a_specs = [pl.BlockSpec((tm, kdim), lambda s: (clip(s - n_stage + 1, n_tiles), 0))]
    else:
        src_split = a_parts[0].shape[0] // tm
        assert 1 <= src_split < n_tiles and a_parts[0].shape[0] % tm == 0
        a_specs = [pl.BlockSpec((tm, kdim), lambda s: (clip(s - n_stage + 1, src_split), 0)),
                   pl.BlockSpec((tm, kdim), lambda s: (
                       clip(s - n_stage + 1 - src_split, n_tiles - src_split), 0))]
    in_specs = a_specs + [_chunk_spec(layer, chunk_rows, d, n_stage),
                          pl.BlockSpec((tm, d), ln_tile),
                          _whole(ln_g), _whole(ln_b)]
    if split_rows is None:
        out_split = None
        out_specs = [pl.BlockSpec((tm, d), ln_tile), pl.BlockSpec((tm, d), ln_tile)]
        out_shape = [jax.ShapeDtypeStruct((m, d), F32), jax.ShapeDtypeStruct((m, d), BF16)]
    else:
        out_split = split_rows // tm
        assert 1 <= out_split < n_tiles and split_rows % tm == 0
        out_specs = [
            pl.BlockSpec((tm, d), lambda s: (clip(s - n_stage, out_split), 0)),
            pl.BlockSpec((tm, d), lambda s: (clip(s - n_stage - out_split, n_tiles - out_split), 0))]
        out_shape = [jax.ShapeDtypeStruct((split_rows, d), F32),
                     jax.ShapeDtypeStruct((m - split_rows, d), F32)]
    return pl.pallas_call(
        functools.partial(_proj_ln_kernel, n_stage=n_stage, n_tiles=n_tiles,
                          src_split=src_split, out_split=out_split, ln_row=ln_row),
        grid=(n_stage + n_tiles,),
        in_specs=in_specs,
        out_specs=out_specs,
        out_shape=out_shape,
        scratch_shapes=[pltpu.VMEM((kdim, d), BF16), pltpu.VMEM((tm, d), F32),
                        pltpu.VMEM((tm, d), F32)],
        compiler_params=_params(("arbitrary",), 60),
        name="proj_ln_split" if split_rows else "proj_ln",
    )(*a_parts, w, x, ln_g, ln_b)


def _rope_table_kernel(cos_ref, sin_ref, *, rows, seq, dec_seq):
    r = lax.broadcasted_iota(jnp.int32, (rows, ROPE_HALF), 0) + pl.program_id(0) * rows
    j = lax.broadcasted_iota(jnp.int32, (rows, ROPE_HALF), 1)
    pos = jnp.where(r < seq, r, PAST_LEN + ((r - seq) & (dec_seq - 1)))
    inv = jnp.power(np.float32(ROPE_BASE), -j.astype(F32) / ROPE_HALF)
    ang = pos.astype(F32) * inv
    cos_ref[...] = jnp.cos(ang)
    sin_ref[...] = jnp.sin(ang)


def _rope_tables(seq, dec_seq, rows):
    assert seq % rows == 0 and rows % dec_seq == 0 and dec_seq & (dec_seq - 1) == 0
    spec = pl.BlockSpec((rows, ROPE_HALF), lambda i: (i, 0))
    return pl.pallas_call(
        functools.partial(_rope_table_kernel, rows=rows, seq=seq, dec_seq=dec_seq),
        grid=(seq // rows + 1,),
        out_specs=[spec, spec],
        out_shape=[jax.ShapeDtypeStruct((seq + rows, ROPE_HALF), F32)] * 2,
        compiler_params=_params(("parallel",), 16),
        name="rope_tables",
    )()


def _ret_in_kernel(x_ref, w_ref, *rest, tn, rope, parts):
    if rope:
        cos_ref, sin_ref, o_ref, w_b = rest
    else:
        o_ref, w_b = rest

    @pl.when(pl.program_id(1) == 0)
    def _():
        w_b[...] = w_ref[...].astype(BF16)

    rows = x_ref.shape[0] // parts
    heads, _, width = o_ref.shape
    for p in range(parts):
        rs = slice(p * rows, (p + 1) * rows)
        acc = jnp.dot(x_ref[rs, :], w_b[...], preferred_element_type=F32)
        if not rope:
            for hh in range(heads):
                o_ref[hh, rs, :] = acc[:, hh * width:(hh + 1) * width].astype(BF16)
            continue
        scale = jnp.where(pl.program_id(0) < RET_QK // tn, np.float32(RET_DK ** -0.5),
                          np.float32(1.0))
        cos = cos_ref[rs, :]
        sin = sin_ref[rs, :]
        for hh in range(heads):
            a = acc[:, hh * RET_DK:hh * RET_DK + ROPE_HALF]
            b = acc[:, hh * RET_DK + ROPE_HALF:(hh + 1) * RET_DK]
            o_ref[hh, rs, :ROPE_HALF] = ((a * cos - b * sin) * scale).astype(BF16)
            o_ref[hh, rs, ROPE_HALF:] = ((a * sin + b * cos) * scale).astype(BF16)


def _ret_in(xb, w, col0, n_cols, width, tm, parts, rope=None, tn=1024):
    m, d = xb.shape
    assert tn % width == 0 and (not rope or width == RET_DK)
    in_specs = [pl.BlockSpec((tm, d), lambda j, i: (i, 0)),
                pl.BlockSpec((None, d, tn), lambda j, i: (0, 0, col0 // tn + j))]
    args = [xb, w]
    if rope:
        cos, sin, n_prompt_rows, seq = rope
        assert cos.shape[0] == seq + tm and n_prompt_rows % tm == 0
        per_seq = seq // tm
        table_tile = lambda j, i: (jnp.where(i < n_prompt_rows // tm, i % per_seq, per_seq), 0)
        in_specs += [pl.BlockSpec((tm, ROPE_HALF), table_tile)] * 2
        args += [cos, sin]
    return pl.pallas_call(
        functools.partial(_ret_in_kernel, tn=tn, rope=bool(rope), parts=parts),
        grid=(n_cols // tn, m // tm),
        in_specs=in_specs,
        out_specs=pl.BlockSpec((tn // width, tm, width), lambda j, i: (j, i, 0)),
        out_shape=jax.ShapeDtypeStruct((n_cols // width, m, width), BF16),
        scratch_shapes=[pltpu.VMEM((d, tn), BF16)],
        compiler_params=_params(("arbitrary", "arbitrary"), 60),
        name="ret_in_qk" if rope else "ret_in_vg",
    )(*args)


def _decay_tables(head, blk):
    def log_gamma(shape):
        hv = jnp.full(shape, head, dtype=jnp.int32).astype(F32)
        return jnp.log1p(-jnp.exp2(-5.0 - hv))

    row = lax.broadcasted_iota(jnp.int32, (blk, blk), 0).astype(F32)
    col = lax.broadcasted_iota(jnp.int32, (blk, blk), 1).astype(F32)
    diff = row - col
    decay = jnp.where(diff >= 0, jnp.exp(log_gamma((blk, blk)) * jnp.maximum(diff, 0.0)), 0.0)
    idx_v = lax.broadcasted_iota(jnp.int32, (blk, RET_DV), 0).astype(F32)
    idx_k = lax.broadcasted_iota(jnp.int32, (blk, RET_DK), 0).astype(F32)
    inner = jnp.exp(log_gamma((blk, RET_DV)) * (idx_v + 1.0))
    zeta = jnp.exp(log_gamma((blk, RET_DK)) * (blk - 1.0 - idx_k))
    blk_decay = jnp.exp(log_gamma((1, RET_DV)) * blk)
    return decay, inner, zeta, blk_decay


def _ret_block(qb, kb, vb, gb, state, tables):
    decay, inner, zeta, blk_decay = tables
    scores = lax.dot_general(qb, kb, (((1,), (1,)), ((), ())),
                             preferred_element_type=F32) * decay
    intra = jnp.dot(scores.astype(BF16), vb, preferred_element_type=F32)
    cross = jnp.dot(qb, state.astype(BF16), preferred_element_type=F32) * inner
    kz = (kb.astype(F32) * zeta).T.astype(BF16)
    new_state = blk_decay * state + jnp.dot(kz, vb, preferred_element_type=F32)
    o = intra + cross
    mu = jnp.mean(o, axis=-1, keepdims=True)
    oc = o - mu
    var = jnp.mean(oc * oc, axis=-1, keepdims=True)
    on = oc * lax.rsqrt(var + GN_EPS)
    y = (jax.nn.silu(gb.astype(F32)) * on).astype(BF16)
    return y, new_state


def _ret_kernel(q_ref, k_ref, v_ref, g_ref, qs_ref, ks_ref, vs_ref, gs_ref, s0_hbm,
                y_ref, ys_ref, sp_ref, ss_ref, ring, sems, *, blk, n_blk, dec_seq, streams):
    step = pl.program_id(0)
    n_steps = pl.num_programs(0)
    slot = lax.rem(step, STATE_RING)

    def state_copy(at_step, at_slot):
        first = pl.multiple_of(at_step * streams, streams)
        return pltpu.make_async_copy(s0_hbm.at[0, pl.ds(first, streams)], ring.at[at_slot],
                                     sems.at[at_slot])

    @pl.when(step == 0)
    def _():
        for k in range(STATE_RING):
            state_copy(k, k).start()

    tables = _decay_tables(pl.program_id(0) % RET_HEADS, blk)
    state = jnp.zeros((RET_DK, RET_DV), F32)
    for t in range(n_blk):
        rows = slice(t * blk, (t + 1) * blk)
        y, state = _ret_block(q_ref[rows, :], k_ref[rows, :], v_ref[rows, :],
                              g_ref[rows, :], state, tables)
        y_ref[rows, :] = y
    sp_ref[0, 0] = state
    state_copy(step, slot).wait()
    for i in range(streams):
        rows = slice(i * dec_seq, (i + 1) * dec_seq)
        for h in range(RET_HEADS):
            cv = slice(h * RET_DV, (h + 1) * RET_DV)
            y, new_state = _ret_block(qs_ref[h, rows, :], ks_ref[h, rows, :], vs_ref[h, rows, :],
                                      gs_ref[h, rows, :], ring[slot, i, h], _decay_tables(h, dec_seq))
            ys_ref[rows, cv] = y
            ss_ref[i, h] = new_state

    @pl.when(step + STATE_RING < n_steps)
    def _():
        state_copy(step + STATE_RING, slot).start()


def _retention(qk, vg, state0, batch, seq, dec_batch, dec_seq):
    blk = min(seq, RET_CHUNK)
    steps = batch * RET_HEADS
    assert dec_seq <= RET_BLOCK and dec_batch % steps == 0 and (batch * seq) % dec_seq == 0
    assert steps >= STATE_RING
    streams = dec_batch // steps
    srows = streams * dec_seq
    r_off = batch * seq // srows
    bh = lambda i: (i // RET_HEADS, i % RET_HEADS)
    state_shape = (RET_HEADS, RET_DK, RET_DV)
    return pl.pallas_call(
        functools.partial(_ret_kernel, blk=blk, n_blk=seq // blk, dec_seq=dec_seq,
                          streams=streams),
        grid=(steps,),
        in_specs=[pl.BlockSpec((None, seq, RET_DK), lambda i: (i % RET_HEADS, i // RET_HEADS, 0)),
                  pl.BlockSpec((None, seq, RET_DK),
                               lambda i: (RET_HEADS + i % RET_HEADS, i // RET_HEADS, 0)),
                  pl.BlockSpec((None, seq, RET_DV), lambda i: (i % RET_HEADS, i // RET_HEADS, 0)),
                  pl.BlockSpec((None, seq, RET_DV),
                               lambda i: (RET_HEADS + i % RET_HEADS, i // RET_HEADS, 0)),
                  pl.BlockSpec((RET_HEADS, srows, RET_DK), lambda i: (0, r_off + i, 0)),
                  pl.BlockSpec((RET_HEADS, srows, RET_DK), lambda i: (1, r_off + i, 0)),
                  pl.BlockSpec((RET_HEADS, srows, RET_DV), lambda i: (0, r_off + i, 0)),
                  pl.BlockSpec((RET_HEADS, srows, RET_DV), lambda i: (1, r_off + i, 0)),
                  pl.BlockSpec(memory_space=pl.ANY)],
        out_specs=[pl.BlockSpec((seq, RET_DV), lambda i: bh(i)),
                   pl.BlockSpec((srows, RET_V), lambda i: (i, 0)),
                   pl.BlockSpec((1, 1, RET_DK, RET_DV), lambda i: bh(i) + (0, 0)),
                   pl.BlockSpec((streams,) + state_shape, lambda i: (i, 0, 0, 0))],
        out_shape=[jax.ShapeDtypeStruct((batch * seq, RET_V), BF16),
                   jax.ShapeDtypeStruct((dec_batch * dec_seq, RET_V), BF16),
                   jax.ShapeDtypeStruct((batch, RET_HEADS, RET_DK, RET_DV), F32),
                   jax.ShapeDtypeStruct((dec_batch,) + state_shape, F32)],
        scratch_shapes=[pltpu.VMEM((STATE_RING, streams) + state_shape, F32),
                        pltpu.SemaphoreType.DMA((STATE_RING,))],
        compiler_params=_params(("arbitrary",), 56),
        name="retention",
    )(qk, qk, vg, vg, qk, qk, vg, vg, state0)


def kernel(x_prompt, x_sample, state_ret, w_a_in, a_ln_g, a_ln_b, a_ws, a_bs, w_a_out,
           w_b_in, w_b_out, w_ffn_in, w_ffn_out, ln_mix_g, ln_mix_b, ln_ffn_g, ln_ffn_b):
    batch, seq, d = x_prompt.shape
    dec_batch, dec_seq, _ = x_sample.shape
    n_p = batch * seq
    n_s = dec_batch * dec_seq
    assert d == D_MODEL and seq % GMLP_CHUNK == 0 and GMLP_CHUNK % dec_seq == 0
    assert w_a_in.shape[0] == 1 and w_b_in.shape[0] == 1 and state_ret.shape[0] == 1

    reps = GMLP_CHUNK // dec_seq
    expand = lambda b: jnp.repeat(b.T, GMLP_GROUP_DIM, axis=1)
    wmix = jnp.stack([a_ws[0], jnp.tile(a_ws[0][:, :dec_seq, :dec_seq], (1, reps, reps))])
    bias = jnp.stack([expand(a_bs[0]), expand(jnp.tile(a_bs[0][:, :dec_seq], (1, reps)))])
    x1, x1b, v_sample = _gmlp_layer(x_prompt.reshape(n_p, d), x_sample.reshape(n_s, d), w_a_in,
                                    w_a_out, a_ln_g, a_ln_b, wmix, bias, ln_mix_g, ln_mix_b, dec_seq)
    act = _ffn_in(x1b, w_ffn_in, 0)
    x2, x2b = _proj_ln(act, w_ffn_out, 0, x1, ln_ffn_g, ln_ffn_b, 0, chunk_rows=512)

    cos, sin = _rope_tables(seq, dec_seq, rows=1024)
    qk = _ret_in(x2b, w_b_in, 0, 2 * RET_QK, RET_DK, tm=1024, parts=1, rope=(cos, sin, n_p, seq))
    vg = _ret_in(x2b, w_b_in, 2 * RET_QK, 2 * RET_V, RET_DV, tm=3072, parts=3)
    y_p, y_s, state_p, state_s = _retention(qk, vg, state_ret, batch, seq, dec_batch, dec_seq)
    x3, x3b = _proj_ln((y_p, y_s), w_b_out, 0, x2, ln_mix_g, ln_mix_b, 1, chunk_rows=512)
    act = _ffn_in(x3b, w_ffn_in, 1)
    y_prompt, y_sample = _proj_ln(act, w_ffn_out, 1, x3, ln_ffn_g, ln_ffn_b, 1, chunk_rows=512,
                                  split_rows=n_p)

    return (y_prompt.reshape(batch, seq, d), y_sample.reshape(dec_batch, dec_seq, d),
            state_p[None], state_s[None], v_sample.reshape(1, dec_batch, dec_seq, d))
```

```python
import functools
import math

import jax
import jax.numpy as jnp
import numpy as np
from jax import lax
from jax.experimental import pallas as pl
from jax.experimental.pallas import tpu as pltpu

D_MODEL = 2048
DEPTH = 2
PAST_LEN = 4096
RET_BLOCK = 64
RET_CHUNK = 256
GMLP_CHUNK = 128
GMLP_GROUPS = 8
GMLP_GROUP_DIM = D_MODEL // GMLP_GROUPS
RET_HEADS = 8
RET_DK = D_MODEL // RET_HEADS
RET_DV = 2 * RET_DK
RET_QK = RET_HEADS * RET_DK
RET_V = RET_HEADS * RET_DV
ROPE_BASE = 10000.0
ROPE_HALF = RET_DK // 2
ALPHA = (2 * DEPTH) ** 0.25
LN_EPS = 1e-5
GN_EPS = 1e-6

F32 = jnp.float32
BF16 = jnp.bfloat16
MIB = 1024 * 1024


def _params(semantics, vmem_mib):
    return pltpu.CompilerParams(dimension_semantics=semantics,
                                vmem_limit_bytes=vmem_mib * MIB)


def _layer_norm(x, g, b):
    mu = jnp.mean(x, axis=-1, keepdims=True)
    xc = x - mu
    var = jnp.mean(xc * xc, axis=-1, keepdims=True)
    return xc * lax.rsqrt(var + LN_EPS) * g + b


def _gelu(x):
    return 0.5 * x * (1.0 + lax.erf(x * np.float32(math.sqrt(0.5))))


def _stage_weight(step, w_ref, wb_ref, n_chunks):
    rows = w_ref.shape[0]

    @pl.when(step < n_chunks)
    def _():
        r0 = pl.multiple_of(step * rows, rows)
        wb_ref[pl.ds(r0, rows), :] = w_ref[...].astype(BF16)


def _whole(a):
    return pl.BlockSpec(a.shape, lambda s: (0,) * a.ndim)


def _chunk_spec(layer, rows, cols, n_chunks):
    return pl.BlockSpec((None, rows, cols),
                        lambda s: (layer, jnp.minimum(s, n_chunks - 1), 0))


def _gmlp_kernel(xp_ref, xs_ref, win_ref, wout_ref, lng_ref, lnb_ref, wmix_ref, bias_ref, g_ref,
                 b_ref, x1_ref, x1b_ref, vn_ref, win_b, wout_b, u_ref, v_ref, s0, s1, xr0, xr1,
                 *, tm, n_stage, n_tiles, n_prompt_tiles, dec_seq):
    step = pl.program_id(0)
    _stage_weight(step, win_ref, win_b, n_stage)
    _stage_weight(step, wout_ref, wout_b, n_stage)
    t = step - n_stage
    d = D_MODEL

    def gate_in(xr_ref, sample):
        xf = (xs_ref if sample else xp_ref)[...]
        xr_ref[...] = xf
        xb = xf.astype(BF16)
        u_ref[...] = _gelu(jnp.dot(xb, win_b[:, :d], preferred_element_type=F32))
        v = _layer_norm(_gelu(jnp.dot(xb, win_b[:, d:], preferred_element_type=F32)),
                        lng_ref[...], lnb_ref[...])
        if sample:
            vn_ref[...] = v
        v_ref[...] = v.astype(BF16)

    def gate_mix(s_ref, sample):
        row = lax.broadcasted_iota(jnp.int32, (GMLP_CHUNK, GMLP_CHUNK), 0)
        col = lax.broadcasted_iota(jnp.int32, (GMLP_CHUNK, GMLP_CHUNK), 1)
        mask = col <= row
        if sample:
            mask = mask & (((row ^ col) & ~(dec_seq - 1)) == 0)
        which = 1 if sample else 0
        for g in range(GMLP_GROUPS):
            w_g = jnp.where(mask, wmix_ref[which, g], 0.0).astype(BF16)
            cs = slice(g * GMLP_GROUP_DIM, (g + 1) * GMLP_GROUP_DIM)
            for seg in range(tm // GMLP_CHUNK):
                rs = slice(seg * GMLP_CHUNK, (seg + 1) * GMLP_CHUNK)
                mixed = jnp.dot(w_g, v_ref[rs, cs], preferred_element_type=F32)
                mixed = mixed + bias_ref[which, :, cs]
                s_ref[rs, cs] = (u_ref[rs, cs] * mixed).astype(BF16)

    def project(s_ref):
        return jnp.dot(s_ref[...], wout_b[...], preferred_element_type=F32)

    def finish(h, xr_ref):
        x1 = _layer_norm(ALPHA * xr_ref[...] + h, g_ref[0:1, :], b_ref[0:1, :])
        x1_ref[...] = x1
        x1b_ref[...] = x1.astype(BF16)

    @pl.when(t == 0)
    def _():
        gate_in(xr0, False)
        gate_mix(s0, False)

    for parity, (cur, prev) in enumerate((((s0, xr0), (s1, xr1)), ((s1, xr1), (s0, xr0)))):
        for sample in (False, True):
            cond = ((t & 1) == parity) & (t >= 1) & (t < n_tiles)
            cond &= (t >= n_prompt_tiles) if sample else (t < n_prompt_tiles)

            @pl.when(cond)
            def _(cur=cur, prev=prev, sample=sample):
                gate_in(cur[1], sample)
                h = project(prev[0])
                gate_mix(cur[0], sample)
                finish(h, prev[1])

    @pl.when(t == n_tiles)
    def _():
        last = (s0, xr0) if (n_tiles - 1) % 2 == 0 else (s1, xr1)
        finish(project(last[0]), last[1])


def _gmlp_layer(xp, xs, w_in, w_out, ln_g, ln_b, wmix, bias, mix_g, mix_b, dec_seq,
                tm=256, n_stage=16):
    (n_p, d), n_s = xp.shape, xs.shape[0]
    n_prompt_tiles, n_sample_tiles = n_p // tm, n_s // tm
    n_tiles = n_prompt_tiles + n_sample_tiles
    assert n_p % tm == 0 and n_s % tm == 0 and n_sample_tiles >= 1
    clip = lambda v, n: jnp.clip(v, 0, n - 1)
    sample_tile = lambda s: (clip(s - n_stage - n_prompt_tiles, n_sample_tiles), 0)
    out_tile = lambda s: (clip(s - n_stage - 1, n_tiles), 0)
    once = pl.Buffered(1)
    in_specs = [pl.BlockSpec((tm, d), lambda s: (clip(s - n_stage, n_prompt_tiles), 0)),
                pl.BlockSpec((tm, d), sample_tile),
                _chunk_spec(0, d // n_stage, 2 * d, n_stage),
                _chunk_spec(0, d // n_stage, d, n_stage),
                _whole(ln_g), _whole(ln_b),
                pl.BlockSpec((2, GMLP_GROUPS, GMLP_CHUNK, GMLP_CHUNK), lambda s: (0, 0, 0, 0),
                             pipeline_mode=once),
                pl.BlockSpec((2, GMLP_CHUNK, d), lambda s: (0, 0, 0), pipeline_mode=once),
                _whole(mix_g), _whole(mix_b)]
    return pl.pallas_call(
        functools.partial(_gmlp_kernel, tm=tm, n_stage=n_stage, n_tiles=n_tiles,
                          n_prompt_tiles=n_prompt_tiles, dec_seq=dec_seq),
        grid=(n_stage + n_tiles + 1,),
        in_specs=in_specs,
        out_specs=[pl.BlockSpec((tm, d), out_tile), pl.BlockSpec((tm, d), out_tile),
                   pl.BlockSpec((tm, d), sample_tile, pipeline_mode=once)],
        out_shape=[jax.ShapeDtypeStruct((n_p + n_s, d), F32),
                   jax.ShapeDtypeStruct((n_p + n_s, d), BF16),
                   jax.ShapeDtypeStruct((n_s, d), F32)],
        scratch_shapes=[pltpu.VMEM((d, 2 * d), BF16), pltpu.VMEM((d, d), BF16),
                        pltpu.VMEM((tm, d), F32), pltpu.VMEM((tm, d), BF16),
                        pltpu.VMEM((tm, d), BF16), pltpu.VMEM((tm, d), BF16),
                        pltpu.VMEM((tm, d), F32), pltpu.VMEM((tm, d), F32)],
        compiler_params=_params(("arbitrary",), 63),
        name="gmlp",
    )(xp, xs, w_in, w_out, ln_g, ln_b, wmix, bias, mix_g, mix_b)


def _ffn_in_kernel(x_ref, wg_ref, wu_ref, o_ref, wg_b, wu_b, *, parts):
    @pl.when(pl.program_id(1) == 0)
    def _():
        wg_b[...] = wg_ref[...].astype(BF16)
        wu_b[...] = wu_ref[...].astype(BF16)

    rows = x_ref.shape[0] // parts
    for p in range(parts):
        rs = slice(p * rows, (p + 1) * rows)
        xb = x_ref[rs, :]
        gate = jnp.dot(xb, wg_b[...], preferred_element_type=F32)
        up = jnp.dot(xb, wu_b[...], preferred_element_type=F32)
        o_ref[rs, :] = (jax.nn.silu(gate) * up).astype(BF16)


def _ffn_in(xb, w_in, layer, tm=3072, th=512, parts=4):
    m, d = xb.shape
    hidden = w_in.shape[2] // 2
    nh = hidden // th
    return pl.pallas_call(
        functools.partial(_ffn_in_kernel, parts=parts),
        grid=(nh, m // tm),
        in_specs=[pl.BlockSpec((tm, d), lambda j, i: (i, 0)),
                  pl.BlockSpec((None, d, th), lambda j, i: (layer, 0, j)),
                  pl.BlockSpec((None, d, th), lambda j, i: (layer, 0, j + nh))],
        out_specs=pl.BlockSpec((tm, th), lambda j, i: (i, j)),
        out_shape=jax.ShapeDtypeStruct((m, hidden), BF16),
        scratch_shapes=[pltpu.VMEM((d, th), BF16), pltpu.VMEM((d, th), BF16)],
        compiler_params=_params(("arbitrary", "arbitrary"), 60),
        name="ffn_in",
    )(xb, w_in, w_in)


def _proj_ln_kernel(*refs, n_stage, n_tiles, src_split, out_split, ln_row):
    n_a = 1 if src_split is None else 2
    a_refs, (w_ref, x_ref, g_ref, b_ref), rest = refs[:n_a], refs[n_a:n_a + 4], refs[n_a + 4:]
    outs, (w_b, h0, h1) = rest[:-3], rest[-3:]
    step = pl.program_id(0)
    t = step - n_stage + 1

    rows = w_ref.shape[0]
    for c in range(n_stage):
        @pl.when(step == c)
        def _(c=c):
            ks = slice(c * rows, (c + 1) * rows)
            w_c = w_ref[...].astype(BF16)
            w_b[ks, :] = w_c
            part = jnp.dot(a_refs[0][:, ks], w_c, preferred_element_type=F32)
            if c == 0:
                h0[...] = part
            else:
                h0[...] += part

    def matmul(dst, second_src):
        a_ref = a_refs[1 if second_src else 0]
        dst[...] = jnp.dot(a_ref[...], w_b[...], preferred_element_type=F32)

    def epilogue(src, second_out):
        ln = slice(ln_row, ln_row + 1)
        x = _layer_norm(ALPHA * x_ref[...] + src[...], g_ref[ln, :], b_ref[ln, :])
        if out_split is None:
            outs[0][...] = x
            outs[1][...] = x.astype(BF16)
        else:
            outs[1 if second_out else 0][...] = x

    for parity, (dst, src) in enumerate(((h0, h1), (h1, h0))):
        for second_src in ((False, True) if src_split is not None else (False,)):
            for second_out in ((False, True) if out_split is not None else (False,)):
                cond = ((t & 1) == parity) & (t >= 1) & (t < n_tiles)
                if src_split is not None:
                    cond &= (t >= src_split) if second_src else (t < src_split)
                if out_split is not None:
                    cond &= (t - 1 >= out_split) if second_out else (t - 1 < out_split)

                @pl.when(cond)
                def _(dst=dst, src=src, second_src=second_src, second_out=second_out):
                    matmul(dst, second_src)
                    epilogue(src, second_out)

    @pl.when(t == n_tiles)
    def _():
        epilogue(h0 if (n_tiles - 1) % 2 == 0 else h1, out_split is not None)


def _proj_ln(a, w, layer, x, ln_g, ln_b, ln_row, chunk_rows, split_rows=None, tm=256):
    a_parts = a if isinstance(a, tuple) else (a,)
    kdim = a_parts[0].shape[1]
    m = sum(p.shape[0] for p in a_parts)
    d = w.shape[2]
    n_stage = kdim // chunk_rows
    n_tiles = m // tm
    clip = lambda v, n: jnp.clip(v, 0, n - 1)
    ln_tile = lambda s: (clip(s - n_stage, n_tiles), 0)
    if len(a_parts) == 1:
        src_split = None
        a_specs = [pl.BlockSpec((tm, kdim), lambda s: (clip(s - n_stage + 1, n_tiles), 0))]
    else:
        src_split = a_parts[0].shape[0] // tm
        assert 1 <= src_split < n_tiles and a_parts[0].shape[0] % tm == 0
        a_specs = [pl.BlockSpec((tm, kdim), lambda s: (clip(s - n_stage + 1, src_split), 0)),
                   pl.BlockSpec((tm, kdim), lambda s: (
                       clip(s - n_stage + 1 - src_split, n_tiles - src_split), 0))]
    in_specs = a_specs + [_chunk_spec(layer, chunk_rows, d, n_stage),
                          pl.BlockSpec((tm, d), ln_tile),
                          _whole(ln_g), _whole(ln_b)]
    if split_rows is None:
        out_split = None
        out_specs = [pl.BlockSpec((tm, d), ln_tile), pl.BlockSpec((tm, d), ln_tile)]
        out_shape = [jax.ShapeDtypeStruct((m, d), F32), jax.ShapeDtypeStruct((m, d), BF16)]
    else:
        out_split = split_rows // tm
        assert 1 <= out_split < n_tiles and split_rows % tm == 0
        out_specs = [
            pl.BlockSpec((tm, d), lambda s: (clip(s - n_stage, out_split), 0)),
            pl.BlockSpec((tm, d), lambda s: (clip(s - n_stage - out_split, n_tiles - out_split), 0))]
        out_shape = [jax.ShapeDtypeStruct((split_rows, d), F32),
                     jax.ShapeDtypeStruct((m - split_rows, d), F32)]
    return pl.pallas_call(
        functools.partial(_proj_ln_kernel, n_stage=n_stage, n_tiles=n_tiles,
                          src_split=src_split, out_split=out_split, ln_row=ln_row),
        grid=(n_stage + n_tiles,),
        in_specs=in_specs,
        out_specs=out_specs,
        out_shape=out_shape,
        scratch_shapes=[pltpu.VMEM((kdim, d), BF16), pltpu.VMEM((tm, d), F32),
                        pltpu.VMEM((tm, d), F32)],
        compiler_params=_params(("arbitrary",), 60),
        name="proj_ln_split" if split_rows else "proj_ln",
    )(*a_parts, w, x, ln_g, ln_b)


def _rope_table_kernel(cos_ref, sin_ref, *, rows, seq, dec_seq):
    r = lax.broadcasted_iota(jnp.int32, (rows, ROPE_HALF), 0) + pl.program_id(0) * rows
    j = lax.broadcasted_iota(jnp.int32, (rows, ROPE_HALF), 1)
    pos = jnp.where(r < seq, r, PAST_LEN + ((r - seq) & (dec_seq - 1)))
    inv = jnp.power(np.float32(ROPE_BASE), -j.astype(F32) / ROPE_HALF)
    ang = pos.astype(F32) * inv
    cos_ref[...] = jnp.cos(ang)
    sin_ref[...] = jnp.sin(ang)


def _rope_tables(seq, dec_seq, rows):
    assert seq % rows == 0 and rows % dec_seq == 0 and dec_seq & (dec_seq - 1) == 0
    spec = pl.BlockSpec((rows, ROPE_HALF), lambda i: (i, 0))
    return pl.pallas_call(
        functools.partial(_rope_table_kernel, rows=rows, seq=seq, dec_seq=dec_seq),
        grid=(seq // rows + 1,),
        out_specs=[spec, spec],
        out_shape=[jax.ShapeDtypeStruct((seq + rows, ROPE_HALF), F32)] * 2,
        compiler_params=_params(("parallel",), 16),
        name="rope_tables",
    )()


def _ret_in_kernel(x_ref, w_ref, *rest, tn, rope, parts):
    if rope:
        cos_ref, sin_ref, o_ref, w_b = rest
    else:
        o_ref, w_b = rest

    @pl.when(pl.program_id(1) == 0)
    def _():
        w_b[...] = w_ref[...].astype(BF16)

    rows = x_ref.shape[0] // parts
    heads, _, width = o_ref.shape
    for p in range(parts):
        rs = slice(p * rows, (p + 1) * rows)
        acc = jnp.dot(x_ref[rs, :], w_b[...], preferred_element_type=F32)
        if not rope:
            for hh in range(heads):
                o_ref[hh, rs, :] = acc[:, hh * width:(hh + 1) * width].astype(BF16)
            continue
        scale = jnp.where(pl.program_id(0) < RET_QK // tn, np.float32(RET_DK ** -0.5),
                          np.float32(1.0))
        cos = cos_ref[rs, :]
        sin = sin_ref[rs, :]
        for hh in range(heads):
            a = acc[:, hh * RET_DK:hh * RET_DK + ROPE_HALF]
            b = acc[:, hh * RET_DK + ROPE_HALF:(hh + 1) * RET_DK]
            o_ref[hh, rs, :ROPE_HALF] = ((a * cos - b * sin) * scale).astype(BF16)
            o_ref[hh, rs, ROPE_HALF:] = ((a * sin + b * cos) * scale).astype(BF16)


def _ret_in(xb, w, col0, n_cols, width, tm, parts, rope=None, tn=1024):
    m, d = xb.shape
    assert tn % width == 0 and (not rope or width == RET_DK)
    in_specs = [pl.BlockSpec((tm, d), lambda j, i: (i, 0)),
                pl.BlockSpec((None, d, tn), lambda j, i: (0, 0, col0 // tn + j))]
    args = [xb, w]
    if rope:
        cos, sin, n_prompt_rows, seq = rope
        assert cos.shape[0] == seq + tm and n_prompt_rows % tm == 0
        per_seq = seq // tm
        table_tile = lambda j, i: (jnp.where(i < n_prompt_rows // tm, i % per_seq, per_seq), 0)
        in_specs += [pl.BlockSpec((tm, ROPE_HALF), table_tile)] * 2
        args += [cos, sin]
    return pl.pallas_call(
        functools.partial(_ret_in_kernel, tn=tn, rope=bool(rope), parts=parts),
        grid=(n_cols // tn, m // tm),
        in_specs=in_specs,
        out_specs=pl.BlockSpec((tn // width, tm, width), lambda j, i: (j, i, 0)),
        out_shape=jax.ShapeDtypeStruct((n_cols // width, m, width), BF16),
        scratch_shapes=[pltpu.VMEM((d, tn), BF16)],
        compiler_params=_params(("arbitrary", "arbitrary"), 60),
        name="ret_in_qk" if rope else "ret_in_vg",
    )(*args)


def _decay_tables(head, blk):
    def log_gamma(shape):
        hv = jnp.full(shape, head, dtype=jnp.int32).astype(F32)
        return jnp.log1p(-jnp.exp2(-5.0 - hv))

    row = lax.broadcasted_iota(jnp.int32, (blk, blk), 0).astype(F32)
    col = lax.broadcasted_iota(jnp.int32, (blk, blk), 1).astype(F32)
    diff = row - col
    decay = jnp.where(diff >= 0, jnp.exp(log_gamma((blk, blk)) * jnp.maximum(diff, 0.0)), 0.0)
    idx_v = lax.broadcasted_iota(jnp.int32, (blk, RET_DV), 0).astype(F32)
    idx_k = lax.broadcasted_iota(jnp.int32, (blk, RET_DK), 0).astype(F32)
    inner = jnp.exp(log_gamma((blk, RET_DV)) * (idx_v + 1.0))
    zeta = jnp.exp(log_gamma((blk, RET_DK)) * (blk - 1.0 - idx_k))
    blk_decay = jnp.exp(log_gamma((1, RET_DV)) * blk)
    return decay, inner, zeta, blk_decay


def _ret_block(qb, kb, vb, gb, state, tables):
    decay, inner, zeta, blk_decay = tables
    scores = lax.dot_general(qb, kb, (((1,), (1,)), ((), ())),
                             preferred_element_type=F32) * decay
    intra = jnp.dot(scores.astype(BF16), vb, preferred_element_type=F32)
    cross = jnp.dot(qb, state.astype(BF16), preferred_element_type=F32) * inner
    kz = (kb.astype(F32) * zeta).T.astype(BF16)
    new_state = blk_decay * state + jnp.dot(kz, vb, preferred_element_type=F32)
    o = intra + cross
    mu = jnp.mean(o, axis=-1, keepdims=True)
    oc = o - mu
    var = jnp.mean(oc * oc, axis=-1, keepdims=True)
    on = oc * lax.rsqrt(var + GN_EPS)
    y = (jax.nn.silu(gb.astype(F32)) * on).astype(BF16)
    return y, new_state


def _ret_kernel(q_ref, k_ref, v_ref, g_ref, qs_ref, ks_ref, vs_ref, gs_ref, s0_ref,
                y_ref, ys_ref, sp_ref, ss_ref, *, blk, n_blk, dec_seq, streams):
    tables = _decay_tables(pl.program_id(0) % RET_HEADS, blk)
    state = jnp.zeros((RET_DK, RET_DV), F32)
    for t in range(n_blk):
        rows = slice(t * blk, (t + 1) * blk)
        y, state = _ret_block(q_ref[rows, :], k_ref[rows, :], v_ref[rows, :],
                              g_ref[rows, :], state, tables)
        y_ref[rows, :] = y
    sp_ref[0, 0] = state
    for i in range(streams):
        rows = slice(i * dec_seq, (i + 1) * dec_seq)
        for h in range(RET_HEADS):
            cv = slice(h * RET_DV, (h + 1) * RET_DV)
            y, new_state = _ret_block(qs_ref[h, rows, :], ks_ref[h, rows, :], vs_ref[h, rows, :],
                                      gs_ref[h, rows, :], s0_ref[0, i, h], _decay_tables(h, dec_seq))
            ys_ref[rows, cv] = y
            ss_ref[i, h] = new_state


def _retention(qk, vg, state0, batch, seq, dec_batch, dec_seq):
    blk = min(seq, RET_CHUNK)
    steps = batch * RET_HEADS
    assert dec_seq <= RET_BLOCK and dec_batch % steps == 0 and (batch * seq) % dec_seq == 0
    streams = dec_batch // steps
    srows = streams * dec_seq
    r_off = batch * seq // srows
    bh = lambda i: (i // RET_HEADS, i % RET_HEADS)
    state_shape = (RET_HEADS, RET_DK, RET_DV)
    return pl.pallas_call(
        functools.partial(_ret_kernel, blk=blk, n_blk=seq // blk, dec_seq=dec_seq,
                          streams=streams),
        grid=(steps,),
        in_specs=[pl.BlockSpec((None, seq, RET_DK), lambda i: (i % RET_HEADS, i // RET_HEADS, 0)),
                  pl.BlockSpec((None, seq, RET_DK),
                               lambda i: (RET_HEADS + i % RET_HEADS, i // RET_HEADS, 0)),
                  pl.BlockSpec((None, seq, RET_DV), lambda i: (i % RET_HEADS, i // RET_HEADS, 0)),
                  pl.BlockSpec((None, seq, RET_DV),
                               lambda i: (RET_HEADS + i % RET_HEADS, i // RET_HEADS, 0)),
                  pl.BlockSpec((RET_HEADS, srows, RET_DK), lambda i: (0, r_off + i, 0)),
                  pl.BlockSpec((RET_HEADS, srows, RET_DK), lambda i: (1, r_off + i, 0)),
                  pl.BlockSpec((RET_HEADS, srows, RET_DV), lambda i: (0, r_off + i, 0)),
                  pl.BlockSpec((RET_HEADS, srows, RET_DV), lambda i: (1, r_off + i, 0)),
                  pl.BlockSpec((1, streams) + state_shape, lambda i: (0, i, 0, 0, 0))],
        out_specs=[pl.BlockSpec((seq, RET_DV), lambda i: bh(i)),
                   pl.BlockSpec((srows, RET_V), lambda i: (i, 0)),
                   pl.BlockSpec((1, 1, RET_DK, RET_DV), lambda i: bh(i) + (0, 0)),
                   pl.BlockSpec((streams,) + state_shape, lambda i: (i, 0, 0, 0))],
        out_shape=[jax.ShapeDtypeStruct((batch * seq, RET_V), BF16),
                   jax.ShapeDtypeStruct((dec_batch * dec_seq, RET_V), BF16),
                   jax.ShapeDtypeStruct((batch, RET_HEADS, RET_DK, RET_DV), F32),
                   jax.ShapeDtypeStruct((dec_batch,) + state_shape, F32)],
        compiler_params=_params(("arbitrary",), 56),
        name="retention",
    )(qk, qk, vg, vg, qk, qk, vg, vg, state0)


def kernel(x_prompt, x_sample, state_ret, w_a_in, a_ln_g, a_ln_b, a_ws, a_bs, w_a_out,
           w_b_in, w_b_out, w_ffn_in, w_ffn_out, ln_mix_g, ln_mix_b, ln_ffn_g, ln_ffn_b):
    batch, seq, d = x_prompt.shape
    dec_batch, dec_seq, _ = x_sample.shape
    n_p = batch * seq
    n_s = dec_batch * dec_seq
    assert d == D_MODEL and seq % GMLP_CHUNK == 0 and GMLP_CHUNK % dec_seq == 0
    assert w_a_in.shape[0] == 1 and w_b_in.shape[0] == 1 and state_ret.shape[0] == 1

    reps = GMLP_CHUNK // dec_seq
    expand = lambda b: jnp.repeat(b.T, GMLP_GROUP_DIM, axis=1)
    wmix = jnp.stack([a_ws[0], jnp.tile(a_ws[0][:, :dec_seq, :dec_seq], (1, reps, reps))])
    bias = jnp.stack([expand(a_bs[0]), expand(jnp.tile(a_bs[0][:, :dec_seq], (1, reps)))])
    x1, x1b, v_sample = _gmlp_layer(x_prompt.reshape(n_p, d), x_sample.reshape(n_s, d), w_a_in,
                                    w_a_out, a_ln_g, a_ln_b, wmix, bias, ln_mix_g, ln_mix_b, dec_seq)
    act = _ffn_in(x1b, w_ffn_in, 0)
    x2, x2b = _proj_ln(act, w_ffn_out, 0, x1, ln_ffn_g, ln_ffn_b, 0, chunk_rows=512)

    cos, sin = _rope_tables(seq, dec_seq, rows=1024)
    qk = _ret_in(x2b, w_b_in, 0, 2 * RET_QK, RET_DK, tm=1024, parts=1, rope=(cos, sin, n_p, seq))
    vg = _ret_in(x2b, w_b_in, 2 * RET_QK, 2 * RET_V, RET_DV, tm=3072, parts=3)
    y_p, y_s, state_p, state_s = _retention(qk, vg, state_ret, batch, seq, dec_batch, dec_seq)
    x3, x3b = _proj_ln((y_p, y_s), w_b_out, 0, x2, ln_mix_g, ln_mix_b, 1, chunk_rows=512)
    act = _ffn_in(x3b, w_ffn_in, 1)
    y_prompt, y_sample = _proj_ln(act, w_ffn_out, 1, x3, ln_ffn_g, ln_ffn_b, 1, chunk_rows=512,
                                  split_rows=n_p)

    return (y_prompt.reshape(batch, seq, d), y_sample.reshape(dec_batch, dec_seq, d),
            state_p[None], state_s[None], v_sample.reshape(1, dec_batch, dec_seq, d))
```
